```python
import jax
import jax.numpy as jnp
from jax import lax
import numpy as np

D_MODEL = 1024
BATCH = 4
SEQ = 8192
DEPTH = 2
DEC_BATCH = 32
DEC_SEQ = 4
PAST_LEN = 16384
PAGE_SIZE = 128

N_META = 16
N_HEADS_A = 8
HEAD_DIM = 64
ATTN_WIDTH = N_HEADS_A * HEAD_DIM
CONV_CHANNELS = D_MODEL // 2
CONV_WIDTH = 3
D_FF = 7 * D_MODEL // 2
N_EXPERTS = 8
TOP_K = 2
Q_BLOCK = 128
RMS_EPS = 1e-6
FORGET_BIAS_INIT = 3.0
IN_SIZES = (ATTN_WIDTH, ATTN_WIDTH, ATTN_WIDTH, N_HEADS_A, CONV_CHANNELS, CONV_CHANNELS, CONV_CHANNELS, D_MODEL, D_MODEL)
N_IN = 3 * ATTN_WIDTH + N_HEADS_A + 3 * CONV_CHANNELS + 2 * D_MODEL
N_DENSE_LAYERS = (DEPTH + 1) // 2
N_MOE_LAYERS = DEPTH // 2

kernel_name = "fox_shortconv_gated_hybrid_step"


def rms_norm(x, g):
    xf = x.astype(jnp.float32)
    xf = xf * lax.rsqrt(jnp.mean(xf * xf, axis=-1, keepdims=True) + RMS_EPS)
    return xf.astype(x.dtype) * g


def in_proj(xn, w_in, b_f):
    z = xn @ w_in
    bounds = np.cumsum((0,) + IN_SIZES)
    q, k, v, f_logit, xin, b_gate, c_gate, g_attn, g_conv = [
        z[..., int(bounds[i]):int(bounds[i + 1])] for i in range(len(IN_SIZES))]
    lead = z.shape[:-1]
    q = q.reshape(lead + (N_HEADS_A, HEAD_DIM))
    k = k.reshape(lead + (N_HEADS_A, HEAD_DIM))
    v = v.reshape(lead + (N_HEADS_A, HEAD_DIM))
    log_f = jax.nn.log_sigmoid((f_logit + b_f).astype(jnp.float32))
    u = c_gate * xin
    return q, k, v, log_f, u, b_gate, g_attn, g_conv


def fox_attend(q, k, v, f_q, f_k, q_pos, k_pos):
    s = jnp.einsum('bqhd,bkhd->bhqk', q, k).astype(jnp.float32) * (HEAD_DIM ** -0.5)
    s = s + jnp.transpose(f_q, (0, 2, 1))[..., :, None] - jnp.transpose(f_k, (0, 2, 1))[..., None, :]
    mask = k_pos[None, :] <= q_pos[:, None]
    s = jnp.where(mask[None, None], s, -jnp.inf)
    p = jax.nn.softmax(s, axis=-1).astype(v.dtype)
    return jnp.einsum('bhqk,bkhd->bqhd', p, v)


def fox_prompt(q, k, v, log_f):
    f_cum = jnp.cumsum(log_f, axis=1)
    n_b, n_tok = q.shape[0], q.shape[1]
    pos = jnp.arange(n_tok, dtype=jnp.int32)
    o_meta = fox_attend(q[:, :N_META], k[:, :N_META], v[:, :N_META],
                        f_cum[:, :N_META], f_cum[:, :N_META], pos[:N_META], pos[:N_META])
    n_blk = (n_tok - N_META) // Q_BLOCK

    def block(i):
        start = N_META + i * Q_BLOCK
        qb = lax.dynamic_slice_in_dim(q, start, Q_BLOCK, axis=1)
        fb = lax.dynamic_slice_in_dim(f_cum, start, Q_BLOCK, axis=1)
        pb = start + jnp.arange(Q_BLOCK, dtype=jnp.int32)
        return fox_attend(qb, k, v, fb, f_cum, pb, pos)

    o = lax.map(block, jnp.arange(n_blk, dtype=jnp.int32))
    o = jnp.moveaxis(o, 0, 1).reshape(n_b, n_blk * Q_BLOCK, N_HEADS_A, HEAD_DIM)
    return jnp.concatenate([o_meta, o], axis=1)


def fox_sample(q, k, v, log_f, cache_k_l, cache_v_l, cache_logf_l, page_table):
    n_b, n_pages = page_table.shape
    past = n_pages * cache_k_l.shape[1]
    k_past = cache_k_l[page_table].reshape(n_b, past, N_HEADS_A, HEAD_DIM)
    v_past = cache_v_l[page_table].reshape(n_b, past, N_HEADS_A, HEAD_DIM)
    lf_past = cache_logf_l[page_table].reshape(n_b, past, N_HEADS_A).astype(jnp.float32)
    k_all = jnp.concatenate([k_past, k], axis=1)
    v_all = jnp.concatenate([v_past, v], axis=1)
    f_cum = jnp.cumsum(jnp.concatenate([lf_past, log_f], axis=1), axis=1)
    k_pos = jnp.arange(past + q.shape[1], dtype=jnp.int32)
    q_pos = k_pos[past:]
    return fox_attend(q, k_all, v_all, f_cum[:, past:], f_cum, q_pos, k_pos)


def short_conv(u, buf, w):
    n_t = u.shape[1]
    u_pad = jnp.concatenate([buf.astype(u.dtype), u], axis=1)
    y = w[0] * u_pad[:, 0:n_t]
    for i in range(1, CONV_WIDTH):
        y = y + w[i] * u_pad[:, i:i + n_t]
    return y, u_pad[:, n_t:]


def merge(o_a, y_conv, g_attn, g_conv, w_out_a, w_out_c, w_o):
    lead = o_a.shape[:2]
    ya = o_a.reshape(lead + (ATTN_WIDTH,)) @ w_out_a
    yc = y_conv @ w_out_c
    return (jax.nn.sigmoid(g_attn) * ya + jax.nn.sigmoid(g_conv) * yc) @ w_o


def swiglu(x, wg, wu, wd):
    return (jax.nn.silu(x @ wg) * (x @ wu)) @ wd


def moe_ffn(x, w_r, wg, wu, wd):
    logits = (x @ w_r).astype(jnp.float32)
    top_v, top_i = lax.top_k(logits, TOP_K)
    gates = jax.nn.softmax(top_v, axis=-1)
    combine = jnp.einsum('btk,btke->bte', gates,
                         jax.nn.one_hot(top_i, N_EXPERTS, dtype=jnp.float32)).astype(x.dtype)
    out = jnp.zeros_like(x)
    for e in range(N_EXPERTS):
        out = out + combine[..., e:e + 1] * swiglu(x, wg[e], wu[e], wd[e])
    return out


def channel_mixer(l, xn, ffn_w_gate, ffn_w_up, ffn_w_down, moe_router, moe_w_gate, moe_w_up, moe_w_down):
    j = l // 2
    if l % 2 == 0:
        return swiglu(xn, ffn_w_gate[j], ffn_w_up[j], ffn_w_down[j])
    return moe_ffn(xn, moe_router[j], moe_w_gate[j], moe_w_up[j], moe_w_down[j])


def setup_inputs(seed: int = 0) -> dict:
    key = jax.random.key(seed)
    ks = jax.random.split(key, 32)
    n_pages = PAST_LEN // PAGE_SIZE
    n_used = DEC_BATCH * n_pages
    n_pool = n_used + max(1, n_used // 4)

    def nrm(k, shape, scale):
        return jax.random.normal(k, shape, jnp.float32) * scale

    def gain(k, shape):
        return 1.0 + 0.02 * jax.random.normal(k, shape, jnp.float32)

    return {
        "x_prompt": nrm(ks[0], (BATCH, SEQ, D_MODEL), 1.0),
        "x_sample": nrm(ks[1], (DEC_BATCH, DEC_SEQ, D_MODEL), 1.0),
        "cache_k": nrm(ks[2], (DEPTH, n_pool, PAGE_SIZE, N_HEADS_A, HEAD_DIM), 1.0),
        "cache_v": nrm(ks[3], (DEPTH, n_pool, PAGE_SIZE, N_HEADS_A, HEAD_DIM), 1.0),
        "cache_logf": jax.nn.log_sigmoid(FORGET_BIAS_INIT + jax.random.normal(ks[4], (DEPTH, n_pool, PAGE_SIZE, N_HEADS_A), jnp.float32)),
        "state_conv": nrm(ks[5], (DEPTH, DEC_BATCH, CONV_WIDTH - 1, CONV_CHANNELS), 1.0),
        "page_table": jax.random.permutation(ks[6], n_pool)[:n_used].reshape(DEC_BATCH, n_pages).astype(jnp.int32),
        "meta_tokens": nrm(ks[7], (N_META, D_MODEL), 1.0),
        "norm_mix": gain(ks[8], (DEPTH, D_MODEL)),
        "w_in": nrm(ks[9], (DEPTH, D_MODEL, N_IN), D_MODEL ** -0.5),
        "b_forget": FORGET_BIAS_INIT + 0.1 * jax.random.normal(ks[10], (DEPTH, N_HEADS_A), jnp.float32),
        "conv_w": nrm(ks[11], (DEPTH, CONV_WIDTH, CONV_CHANNELS), CONV_WIDTH ** -0.5),
        "w_out_attn": nrm(ks[12], (DEPTH, ATTN_WIDTH, D_MODEL), ATTN_WIDTH ** -0.5),
        "w_out_conv": nrm(ks[13], (DEPTH, CONV_CHANNELS, D_MODEL), CONV_CHANNELS ** -0.5),
        "w_o": nrm(ks[14], (DEPTH, D_MODEL, D_MODEL), D_MODEL ** -0.5),
        "norm_ffn": gain(ks[15], (DEPTH, D_MODEL)),
        "ffn_w_gate": nrm(ks[16], (N_DENSE_LAYERS, D_MODEL, D_FF), D_MODEL ** -0.5),
        "ffn_w_up": nrm(ks[17], (N_DENSE_LAYERS, D_MODEL, D_FF), D_MODEL ** -0.5),
        "ffn_w_down": nrm(ks[18], (N_DENSE_LAYERS, D_FF, D_MODEL), D_FF ** -0.5),
        "moe_router": nrm(ks[19], (N_MOE_LAYERS, D_MODEL, N_EXPERTS), D_MODEL ** -0.5),
        "moe_w_gate": nrm(ks[20], (N_MOE_LAYERS, N_EXPERTS, D_MODEL, D_FF), D_MODEL ** -0.5),
        "moe_w_up": nrm(ks[21], (N_MOE_LAYERS, N_EXPERTS, D_MODEL, D_FF), D_MODEL ** -0.5),
        "moe_w_down": nrm(ks[22], (N_MOE_LAYERS, N_EXPERTS, D_FF, D_MODEL), D_FF ** -0.5),
        "norm_final": gain(ks[23], (D_MODEL,)),
    }


def reference(x_prompt, x_sample, cache_k, cache_v, cache_logf, state_conv, page_table,
              meta_tokens, norm_mix, w_in, b_forget, conv_w, w_out_attn, w_out_conv, w_o,
              norm_ffn, ffn_w_gate, ffn_w_up, ffn_w_down, moe_router, moe_w_gate, moe_w_up,
              moe_w_down, norm_final):
    n_b = x_prompt.shape[0]
    meta = jnp.broadcast_to(meta_tokens[None].astype(x_prompt.dtype), (n_b, N_META, x_prompt.shape[2]))
    x_p = jnp.concatenate([meta, x_prompt], axis=1)
    x_s = x_sample
    kp, vp, lfp, cp, ks_, vs_, lfs, cs = [], [], [], [], [], [], [], []
    for l in range(DEPTH):
        xn = rms_norm(x_p, norm_mix[l])
        q, k, v, lf, u, bg, ga, gc = in_proj(xn, w_in[l], b_forget[l])
        o_a = fox_prompt(q, k, v, lf)
        conv0 = jnp.zeros((u.shape[0], CONV_WIDTH - 1, CONV_CHANNELS), u.dtype)
        cz, conv_st = short_conv(u, conv0, conv_w[l])
        x_p = x_p + merge(o_a, bg * cz, ga, gc, w_out_attn[l], w_out_conv[l], w_o[l])
        kp.append(k); vp.append(v); lfp.append(lf); cp.append(conv_st)
        xn = rms_norm(x_s, norm_mix[l])
        q, k, v, lf, u, bg, ga, gc = in_proj(xn, w_in[l], b_forget[l])
        o_a = fox_sample(q, k, v, lf, cache_k[l], cache_v[l], cache_logf[l], page_table)
        cz, conv_st = short_conv(u, state_conv[l], conv_w[l])
        x_s = x_s + merge(o_a, bg * cz, ga, gc, w_out_attn[l], w_out_conv[l], w_o[l])
        ks_.append(k); vs_.append(v); lfs.append(lf); cs.append(conv_st)
        x_p = x_p + channel_mixer(l, rms_norm(x_p, norm_ffn[l]), ffn_w_gate, ffn_w_up, ffn_w_down,
                                  moe_router, moe_w_gate, moe_w_up, moe_w_down)
        x_s = x_s + channel_mixer(l, rms_norm(x_s, norm_ffn[l]), ffn_w_gate, ffn_w_up, ffn_w_down,
                                  moe_router, moe_w_gate, moe_w_up, moe_w_down)
    y_prompt = rms_norm(x_p[:, N_META:], norm_final)
    y_sample = rms_norm(x_s, norm_final)
    return (y_prompt, y_sample,
            jnp.stack(kp), jnp.stack(vp), jnp.stack(lfp), jnp.stack(cp),
            jnp.stack(ks_), jnp.stack(vs_), jnp.stack(lfs), jnp.stack(cs))
```

```python
import functools

import jax
import jax.numpy as jnp
from jax import lax
from jax.experimental import pallas as pl
from jax.experimental.pallas import tpu as pltpu

RMS_EPS = 1e-6
TOP_K = 2
NEG_BIG = -1e30
LANES = 128
V7X_VMEM_LIMIT_BYTES = 56 * 1024 * 1024

F32 = jnp.float32
BF16 = jnp.bfloat16


def _pick_tile(n, target, mult=8):
    best = None
    for t in range(mult, min(n, target) + 1, mult):
        if n % t == 0:
            best = t
    assert best is not None, (n, target, mult)
    return best


def _params(sem, vmem=V7X_VMEM_LIMIT_BYTES):
    return pltpu.CompilerParams(dimension_semantics=sem, vmem_limit_bytes=vmem)


def _rms(x, g):
    ms = jnp.mean(x * x, axis=-1, keepdims=True)
    return (x * lax.rsqrt(ms + RMS_EPS)) * g


def _log_sigmoid(x):
    return jnp.minimum(x, 0.0) - jnp.log1p(jnp.exp(-jnp.abs(x)))


def _split3(x):
    hi = x.astype(BF16)
    r1 = x - hi.astype(F32)
    mid = r1.astype(BF16)
    lo = (r1 - mid.astype(F32)).astype(BF16)
    return hi, mid, lo


def _dot(a, b):
    return jnp.dot(a, b, preferred_element_type=F32)


def _dot3(x, w_bf16):
    hi, mid, lo = _split3(x)
    return _dot(hi, w_bf16) + _dot(mid, w_bf16) + _dot(lo, w_bf16)


def _inproj_kernel(*refs, tm, A, C, D, H, scale, halo, sample, tiles_per_seq, T, cst_tile, cst_row):
    if sample:
        (x_ref, g_ref, w_ref, bf_ref, cw_ref, *halo_refs) = refs[: 5 + halo]
        (q_ref, k_ref, v_ref, kb_ref, vb_ref, lf_ref, cy_ref, sga_ref, sgc_ref, u_ref) = refs[5 + halo:]
    else:
        (x_ref, g_ref, w_ref, bf_ref, cw_ref,
         q_ref, k_ref, v_ref, kb_ref, vb_ref, lf_ref, cy_ref, sga_ref, sgc_ref, cst_ref, carry_scr) = refs

    xb = _rms(x_ref[...], g_ref[...]).astype(BF16)

    def mm(lo, hi):
        return _dot(xb, w_ref[:, lo:hi])

    zq = mm(0, 3 * A)
    q_ref[...] = (zq[:, :A] * scale).astype(BF16)
    k = zq[:, A:2 * A]
    v = zq[:, 2 * A:3 * A]
    k_ref[...] = k
    v_ref[...] = v
    kb_ref[...] = k.astype(BF16)
    vb_ref[...] = v.astype(BF16)

    o = 3 * A
    zc = mm(o, o + 3 * C)
    xin, bg, cg = zc[:, :C], zc[:, C:2 * C], zc[:, 2 * C:]
    u = cg * xin
    row = lax.broadcasted_iota(jnp.int32, (tm, C), 0)
    shifted = [u]
    if sample:
        pos = row % T
        for s in range(1, halo + 1):
            shifted.append(jnp.where(pos >= s, pltpu.roll(u, s, 0), halo_refs[s - 1][...]))
        u_ref[...] = u
    else:
        t = pl.program_id(0) % tiles_per_seq

        @pl.when(t == 0)
        def _():
            carry_scr[...] = jnp.zeros_like(carry_scr)

        for s in range(1, halo + 1):
            us = pltpu.roll(u, s, 0)
            for r in range(s):
                us = jnp.where(row == r, carry_scr[halo - s + r:halo - s + r + 1, :], us)
            shifted.append(us)
    y = cw_ref[0:1, :] * shifted[halo]
    for i in range(1, halo + 1):
        y = y + cw_ref[i:i + 1, :] * shifted[halo - i]
    cy_ref[...] = (bg * y).astype(BF16)
    if not sample:
        carry_scr[0:halo, :] = u[tm - halo:tm, :]

        @pl.when(t == cst_tile)
        def _():
            cst_ref[0] = u[cst_row:cst_row + halo, :]

    o += 3 * C
    zg = mm(o, o + 2 * D)
    sga_ref[...] = jax.nn.sigmoid(zg[:, :D])
    sgc_ref[...] = jax.nn.sigmoid(zg[:, D:])

    o += 2 * D
    zf = mm(o, o + LANES)
    lf_ref[...] = _log_sigmoid(zf + bf_ref[...])[:, :H]


def _inproj(x, g, w, bf, cw, *, A, C, H, scale, seq_len=None, valid_len=None, T=None, halos=None, tm_target=384):
    M, D = x.shape
    halo = cw.shape[0] - 1
    sample = halos is not None
    if sample:
        tm = M
        tiles_per_seq, cst_tile, cst_row = 1, 0, 0
        assert T >= halo
    else:
        tm = _pick_tile(seq_len, tm_target, 16)
        tiles_per_seq = seq_len // tm
        cst_tile, cst_row = divmod(valid_len - halo, tm)
        assert cst_row + halo <= tm
    n_seq = M // (tiles_per_seq * tm)
    Nw = w.shape[1]

    def row_spec(n):
        return pl.BlockSpec((tm, n), lambda i: (i, 0))

    def full_spec(a):
        return pl.BlockSpec(a.shape, lambda i: (0,) * a.ndim)

    in_specs = [row_spec(D), full_spec(g), pl.BlockSpec((D, Nw), lambda i: (0, 0), pipeline_mode=pl.Buffered(1)),
                full_spec(bf), full_spec(cw)]
    args = [x, g, w, bf, cw]
    if sample:
        in_specs += [row_spec(C) for _ in halos]
        args += list(halos)
    out_shape = [jax.ShapeDtypeStruct((M, A), BF16), jax.ShapeDtypeStruct((M, A), F32), jax.ShapeDtypeStruct((M, A), F32),
                 jax.ShapeDtypeStruct((M, A), BF16), jax.ShapeDtypeStruct((M, A), BF16), jax.ShapeDtypeStruct((M, H), F32),
                 jax.ShapeDtypeStruct((M, C), BF16), jax.ShapeDtypeStruct((M, D), F32), jax.ShapeDtypeStruct((M, D), F32)]
    out_specs = [row_spec(A)] * 5 + [row_spec(H), row_spec(C), row_spec(D), row_spec(D)]
    scratch = []
    if sample:
        out_shape.append(jax.ShapeDtypeStruct((M, C), F32))
        out_specs.append(row_spec(C))
    else:
        out_shape.append(jax.ShapeDtypeStruct((n_seq, halo, C), F32))
        out_specs.append(pl.BlockSpec((1, halo, C), lambda i: (i // tiles_per_seq, 0, 0)))
        scratch.append(pltpu.VMEM((8, C), F32))
    kern = functools.partial(_inproj_kernel, tm=tm, A=A, C=C, D=D, H=H, scale=scale, halo=halo, sample=sample,
                             tiles_per_seq=tiles_per_seq, T=T, cst_tile=cst_tile, cst_row=cst_row)
    return pl.pallas_call(
        kern, grid=(M // tm,), in_specs=in_specs, out_specs=out_specs, out_shape=out_shape,
        scratch_shapes=scratch, compiler_params=_params(("arbitrary",)),
        name="inproj_sample" if sample else "inproj_prompt")(*args)


def _tri_and_ones():
    i = lax.broadcasted_iota(jnp.int32, (LANES, LANES), 0)
    j = lax.broadcasted_iota(jnp.int32, (LANES, LANES), 1)
    return jnp.where(i <= j, 1.0, 0.0).astype(BF16), jnp.ones((LANES, LANES), BF16)


def _cumsum_kernel(x_ref, o_ref, *, n_chunks, R):
    tri, ones = _tri_and_ones()
    x = x_ref[...].reshape(n_chunks * R, LANES)
    hi, mid, lo = _split3(x)
    y = _dot(hi, tri) + _dot(mid, tri) + _dot(lo, tri)
    tot = _dot(hi, ones) + _dot(mid, ones) + _dot(lo, ones)
    carry = jnp.zeros((R, LANES), F32)
    for c in range(n_chunks):
        o_ref[c] = y[c * R:(c + 1) * R] + carry
        carry = carry + tot[c * R:(c + 1) * R]


def _cumsum_chunks(x):
    n_chunks, R, _ = x.shape
    return pl.pallas_call(
        functools.partial(_cumsum_kernel, n_chunks=n_chunks, R=R),
        out_shape=jax.ShapeDtypeStruct(x.shape, F32), name="logf_cumsum")(x)


def _attn_kernel(q_ref, k_ref, v_ref, fk_ref, o_ref, *, tq, tk, H, Dh):
    qi = pl.program_id(1)
    r = tq // tk
    n_full = qi * r
    contract_last = (((1,), (1,)), ((), ()))
    for h in range(H):
        cols = slice(h * Dh, (h + 1) * Dh)
        q_h = q_ref[0, :, cols]

        def chunk(j, carry, masked, cols=cols, q_h=q_h, h=h):
            m, l, acc = carry
            start = pl.multiple_of(j * tk, tk)
            kc = k_ref[0, pl.ds(start, tk), cols]
            vc = v_ref[0, pl.ds(start, tk), cols]
            s = lax.dot_general(q_h, kc, contract_last, preferred_element_type=F32)
            s = s - fk_ref[0, h:h + 1, pl.ds(start, tk)]
            if masked:
                qpos = qi * tq + lax.broadcasted_iota(jnp.int32, (tq, tk), 0)
                kpos = start + lax.broadcasted_iota(jnp.int32, (tq, tk), 1)
                s = jnp.where(kpos <= qpos, s, NEG_BIG)
            m_new = jnp.maximum(m, jnp.max(s, axis=1, keepdims=True))
            alpha = jnp.exp(m - m_new)
            p = jnp.exp(s - m_new)
            l = alpha * l + jnp.sum(p, axis=1, keepdims=True)
            acc = alpha * acc + _dot(p.astype(BF16), vc)
            return m_new, l, acc

        carry = (jnp.full((tq, 1), NEG_BIG, F32), jnp.zeros((tq, 1), F32), jnp.zeros((tq, Dh), F32))
        carry = lax.fori_loop(0, n_full, functools.partial(chunk, masked=False), carry)
        for d in range(r):
            carry = chunk(n_full + d, carry, True)
        _, l, acc = carry
        o_ref[0, :, cols] = (acc / l).astype(BF16)


def _attn_prompt(q, k, v, fk, *, H, Dh, tq=256, tk=256):
    B, Lp, A = q.shape
    kv_spec = pl.BlockSpec((1, Lp, A), lambda b, i: (b, 0, 0), pipeline_mode=pl.Buffered(1))
    return pl.pallas_call(
        functools.partial(_attn_kernel, tq=tq, tk=tk, H=H, Dh=Dh),
        grid=(B, Lp // tq),
        in_specs=[pl.BlockSpec((1, tq, A), lambda b, i: (b, i, 0)), kv_spec, kv_spec,
                  pl.BlockSpec((1, H, Lp), lambda b, i: (b, 0, 0))],
        out_specs=pl.BlockSpec((1, tq, A), lambda b, i: (b, i, 0)),
        out_shape=jax.ShapeDtypeStruct((B, Lp, A), BF16),
        compiler_params=_params(("arbitrary", "arbitrary")), name="attn_prompt")(q, k, v, fk)


def _attn_sample_kernel(pt_ref, qbd_ref, knew_ref, vnew_ref, lfnew_ref, *refs, PP, T, H, P):
    k_refs, v_refs, lf_refs = refs[:PP], refs[PP:2 * PP], refs[2 * PP:3 * PP]
    o_ref, m_scr, l_scr, acc_scr, carry_scr = refs[3 * PP:]
    j = pl.program_id(1)
    R = T * H
    contract_last = (((1,), (1,)), ((), ()))

    @pl.when(j == 0)
    def _():
        m_scr[...] = jnp.full_like(m_scr, NEG_BIG)
        l_scr[...] = jnp.zeros_like(l_scr)
        acc_scr[...] = jnp.zeros_like(acc_scr)
        carry_scr[...] = jnp.zeros_like(carry_scr)

    tri_ones = jnp.concatenate(_tri_and_ones(), axis=1)
    qbd = qbd_ref[0]

    def page(state, lf, kb, vb, mask):
        m, l, acc, carry = state
        pad = (-H) % 16
        lf_p = jnp.concatenate([lf, jnp.zeros((pad, P), F32)], axis=0) if pad else lf
        y = _dot(jnp.concatenate(_split3(lf_p), axis=0), tri_ones)
        y = y[0:H] + y[H + pad:2 * H + pad] + y[2 * (H + pad):3 * H + 2 * pad]
        fk = carry + y[:, :P]
        carry = carry + y[:, P:]
        s = lax.dot_general(qbd, kb, contract_last, preferred_element_type=F32)
        s = s - jnp.concatenate([fk] * T, axis=0)
        if mask is not None:
            s = jnp.where(mask, s, NEG_BIG)
        m_new = jnp.maximum(m, jnp.max(s, axis=1, keepdims=True))
        alpha = jnp.exp(m - m_new)
        p = jnp.exp(s - m_new)
        l = alpha * l + jnp.sum(p, axis=1, keepdims=True)
        acc = alpha * acc + _dot(p.astype(BF16), vb)
        return m_new, l, acc, carry

    state = (m_scr[...], l_scr[...], acc_scr[...], carry_scr[...])
    for p in range(PP):
        state = page(state, lf_refs[p][0], k_refs[p][0].astype(BF16), v_refs[p][0].astype(BF16), None)
    m_scr[...], l_scr[...], acc_scr[...], carry_scr[...] = state

    @pl.when(j == pl.num_programs(1) - 1)
    def _():
        t_row = lax.broadcasted_iota(jnp.int32, (R, P), 0) // H
        key = lax.broadcasted_iota(jnp.int32, (R, P), 1)
        _, l, acc, _ = page(state, lfnew_ref[0], knew_ref[0], vnew_ref[0], key <= t_row)
        o_ref[0] = acc / l


def _attn_sample(page_table, qbd, k_new, v_new, lf_new, cache_k, cache_v, cache_lf, *, T, H, pages_per_step=8):
    n_seq, n_pages = page_table.shape
    _, P, A = cache_k.shape
    R = T * H
    PP = _pick_tile(n_pages, pages_per_step, 1)

    def seq_spec(shape):
        return pl.BlockSpec((1,) + shape, lambda s, j, pt: (s, 0, 0))

    def page_spec(shape, p):
        return pl.BlockSpec((1,) + shape, lambda s, j, pt: (pt[s, j * PP + p], 0, 0))

    in_specs = ([seq_spec((R, A)), seq_spec((P, A)), seq_spec((P, A)), seq_spec((H, P))]
                + [page_spec((P, A), p) for p in range(PP)] * 2
                + [page_spec((H, P), p) for p in range(PP)])
    grid_spec = pltpu.PrefetchScalarGridSpec(
        num_scalar_prefetch=1, grid=(n_seq, n_pages // PP), in_specs=in_specs,
        out_specs=pl.BlockSpec((1, R, A), lambda s, j, pt: (s, 0, 0)),
        scratch_shapes=[pltpu.VMEM((R, 1), F32), pltpu.VMEM((R, 1), F32), pltpu.VMEM((R, A), F32),
                        pltpu.VMEM((H, P), F32)])
    return pl.pallas_call(
        functools.partial(_attn_sample_kernel, PP=PP, T=T, H=H, P=P), grid_spec=grid_spec,
        out_shape=jax.ShapeDtypeStruct((n_seq, R, A), F32),
        compiler_params=_params(("arbitrary", "arbitrary")), name="attn_sample")(
            page_table, qbd, k_new, v_new, lf_new, *([cache_k] * PP), *([cache_v] * PP), *([cache_lf] * PP))


def _merge_kernel(o_ref, cy_ref, sga_ref, sgc_ref, x_ref, wa_ref, wc_ref, wo_ref, out_ref):
    ya = _dot(o_ref[...], wa_ref[...])
    yc = _dot(cy_ref[...], wc_ref[...])
    mix = (sga_ref[...] * ya + sgc_ref[...] * yc).astype(BF16)
    out_ref[...] = x_ref[...] + _dot(mix, wo_ref[...])


def _merge(o, cy, sga, sgc, x, wa, wc, wo, *, tm_target=512):
    M, D = x.shape
    tm = _pick_tile(M, tm_target, 16)

    def row_spec(n):
        return pl.BlockSpec((tm, n), lambda i: (i, 0))

    def w_spec(a):
        return pl.BlockSpec(a.shape, lambda i: (0, 0))

    return pl.pallas_call(
        _merge_kernel, grid=(M // tm,),
        in_specs=[row_spec(o.shape[1]), row_spec(cy.shape[1]), row_spec(D), row_spec(D), row_spec(D),
                  w_spec(wa), w_spec(wc), w_spec(wo)],
        out_specs=row_spec(D), out_shape=jax.ShapeDtypeStruct((M, D), F32),
        compiler_params=_params(("arbitrary",)), name="merge")(o, cy, sga, sgc, x, wa, wc, wo)


def _swiglu_step(xb, wg, wu, wd, row_scale=None):
    gate = _dot(xb, wg)
    up = _dot(xb, wu)
    h = jax.nn.silu(gate) * up
    if row_scale is not None:
        h = h * row_scale
    return _dot(h.astype(BF16), wd)


def _finish(x, acc, gfin_ref, out_ref):
    y = x + acc
    if gfin_ref is not None:
        y = _rms(y, gfin_ref[...])
    out_ref[...] = y


def _ffn_kernel(*refs, final):
    if final:
        x_ref, g_ref, wg_ref, wu_ref, wd_ref, gfin_ref, out_ref, xn_scr, acc_scr = refs
    else:
        x_ref, g_ref, wg_ref, wu_ref, wd_ref, out_ref, xn_scr, acc_scr = refs
        gfin_ref = None
    f = pl.program_id(1)

    @pl.when(f == 0)
    def _():
        xn_scr[...] = _rms(x_ref[...], g_ref[...]).astype(BF16)
        acc_scr[...] = jnp.zeros_like(acc_scr)

    acc_scr[...] += _swiglu_step(xn_scr[...], wg_ref[...], wu_ref[...], wd_ref[...])

    @pl.when(f == pl.num_programs(1) - 1)
    def _():
        _finish(x_ref[...], acc_scr[...], gfin_ref, out_ref)


def _ffn(x, g, wg, wu, wd, gfin=None, *, tm_target=1024, tf_target=512):
    M, D = x.shape
    F = wg.shape[1]
    tm = _pick_tile(M, tm_target, 16)
    tf = _pick_tile(F, tf_target, LANES)
    final = gfin is not None
    in_specs = [pl.BlockSpec((tm, D), lambda i, f: (i, 0)), pl.BlockSpec((1, D), lambda i, f: (0, 0)),
                pl.BlockSpec((D, tf), lambda i, f: (0, f)), pl.BlockSpec((D, tf), lambda i, f: (0, f)),
                pl.BlockSpec((tf, D), lambda i, f: (f, 0))]
    args = [x, g, wg, wu, wd]
    if final:
        in_specs.append(pl.BlockSpec((1, D), lambda i, f: (0, 0)))
        args.append(gfin)
    return pl.pallas_call(
        functools.partial(_ffn_kernel, final=final), grid=(M // tm, F // tf), in_specs=in_specs,
        out_specs=pl.BlockSpec((tm, D), lambda i, f: (i, 0)), out_shape=jax.ShapeDtypeStruct((M, D), F32),
        scratch_shapes=[pltpu.VMEM((tm, D), BF16), pltpu.VMEM((tm, D), F32)],
        compiler_params=_params(("arbitrary", "arbitrary")), name="ffn_dense")(*args)


def _route(xn, wr_hi, wr_lo, n_experts):
    hi = xn.astype(BF16)
    lo = (xn - hi.astype(F32)).astype(BF16)
    logits = _dot(hi, wr_hi) + _dot(lo, wr_hi) + _dot(hi, wr_lo)
    lane = lax.broadcasted_iota(jnp.int32, logits.shape, 1).astype(F32)
    lg = jnp.where(lane < n_experts, logits, NEG_BIG)
    v1 = jnp.max(lg, axis=1, keepdims=True)
    i1 = jnp.min(jnp.where(lg == v1, lane, float(LANES)), axis=1, keepdims=True)
    lg2 = jnp.where(lane == i1, NEG_BIG, lg)
    v2 = jnp.max(lg2, axis=1, keepdims=True)
    i2 = jnp.min(jnp.where(lg2 == v2, lane, float(LANES)), axis=1, keepdims=True)
    e2 = jnp.exp(v2 - v1)
    den = 1.0 + e2
    return jnp.where(lane == i1, 1.0 / den, 0.0) + jnp.where(lane == i2, e2 / den, 0.0)


def _moe_dense_kernel(*refs, final, n_experts):
    if final:
        x_ref, g_ref, wrh_ref, wrl_ref, wg_ref, wu_ref, wd_ref, gfin_ref, out_ref, xn_scr, acc_scr, comb_scr = refs
    else:
        x_ref, g_ref, wrh_ref, wrl_ref, wg_ref, wu_ref, wd_ref, out_ref, xn_scr, acc_scr, comb_scr = refs
        gfin_ref = None
    e = pl.program_id(1)
    f = pl.program_id(2)

    @pl.when((e == 0) & (f == 0))
    def _():
        xn = _rms(x_ref[...], g_ref[...])
        xn_scr[...] = xn.astype(BF16)
        comb_scr[...] = _route(xn, wrh_ref[...], wrl_ref[...], n_experts)
        acc_scr[...] = jnp.zeros_like(acc_scr)

    comb = comb_scr[...]
    lane = lax.broadcasted_iota(jnp.int32, comb.shape, 1)
    ce = jnp.sum(jnp.where(lane == e, comb, 0.0), axis=1, keepdims=True)
    acc_scr[...] += _swiglu_step(xn_scr[...], wg_ref[0], wu_ref[0], wd_ref[0], row_scale=ce)

    @pl.when((e == pl.num_programs(1) - 1) & (f == pl.num_programs(2) - 1))
    def _():
        _finish(x_ref[...], acc_scr[...], gfin_ref, out_ref)


def _moe_dense(x, g, wr_hi, wr_lo, wg, wu, wd, gfin=None, *, tm_target=1024, tf_target=512):
    M, D = x.shape
    E, _, F = wg.shape
    tm = _pick_tile(M, tm_target, 16)
    tf = _pick_tile(F, tf_target, LANES)
    final = gfin is not None
    vec = pl.BlockSpec((1, D), lambda i, e, f: (0, 0))
    wr_spec = pl.BlockSpec((D, LANES), lambda i, e, f: (0, 0))
    in_specs = [pl.BlockSpec((tm, D), lambda i, e, f: (i, 0)), vec, wr_spec, wr_spec,
                pl.BlockSpec((1, D, tf), lambda i, e, f: (e, 0, f)), pl.BlockSpec((1, D, tf), lambda i, e, f: (e, 0, f)),
                pl.BlockSpec((1, tf, D), lambda i, e, f: (e, f, 0))]
    args = [x, g, wr_hi, wr_lo, wg, wu, wd]
    if final:
        in_specs.append(vec)
        args.append(gfin)
    return pl.pallas_call(
        functools.partial(_moe_dense_kernel, final=final, n_experts=E), grid=(M // tm, E, F // tf),
        in_specs=in_specs, out_specs=pl.BlockSpec((tm, D), lambda i, e, f: (i, 0)),
        out_shape=jax.ShapeDtypeStruct((M, D), F32),
        scratch_shapes=[pltpu.VMEM((tm, D), BF16), pltpu.VMEM((tm, D), F32), pltpu.VMEM((tm, LANES), F32)],
        compiler_params=_params(("arbitrary", "arbitrary", "arbitrary")), name="moe_dense")(*args)


def kernel(x_prompt, x_sample, cache_k, cache_v, cache_logf, state_conv, page_table, meta_tokens, norm_mix, w_in,
           b_forget, conv_w, w_out_attn, w_out_conv, w_o, norm_ffn, ffn_w_gate, ffn_w_up, ffn_w_down, moe_router,
           moe_w_gate, moe_w_up, moe_w_down, norm_final):
    B, S, D = x_prompt.shape
    NB, T, _ = x_sample.shape
    depth, n_pool, P, H, Dh = cache_k.shape
    n_meta = meta_tokens.shape[0]
    A = H * Dh
    C = conv_w.shape[2]
    halo = conv_w.shape[1] - 1
    E = moe_router.shape[2]
    L = S + n_meta
    Lp = -(-L // 256) * 256
    n_chunks = Lp // LANES
    scale = float(Dh) ** -0.5

    o_f = 3 * A
    o_c = o_f + H

    def prep_w_in(w):
        wf = jnp.pad(w[:, o_f:o_c], ((0, 0), (0, LANES - H)))
        return jnp.concatenate([w[:, :o_f], w[:, o_c:], wf], axis=1).astype(BF16)

    def row(v):
        return v.reshape(1, -1)

    meta = jnp.broadcast_to(meta_tokens[None].astype(x_prompt.dtype), (B, n_meta, D))
    x_p = jnp.concatenate([meta, x_prompt, jnp.zeros((B, Lp - L, D), x_prompt.dtype)], axis=1).reshape(B * Lp, D)
    x_s = x_sample.reshape(NB * T, D)
    eye_h = jnp.eye(H, dtype=F32)

    outs = {n: [] for n in ("kp", "vp", "lfp", "cp", "ks", "vs", "lfs", "cs")}
    for l in range(depth):
        last = l == depth - 1
        w_l = prep_w_in(w_in[l])
        bf_l = jnp.pad(b_forget[l], (0, LANES - H)).reshape(1, LANES)
        g_l = row(norm_mix[l])
        wa, wc, wo = w_out_attn[l].astype(BF16), w_out_conv[l].astype(BF16), w_o[l].astype(BF16)

        q, k, v, kb, vb, lf, cy, sga, sgc, cst = _inproj(
            x_p, g_l, w_l, bf_l, conv_w[l], A=A, C=C, H=H, scale=scale, seq_len=Lp, valid_len=L)
        lf_chunks = lf.reshape(B, n_chunks, LANES, H).transpose(1, 0, 3, 2).reshape(n_chunks, B * H, LANES)
        fk = _cumsum_chunks(lf_chunks).reshape(n_chunks, B, H, LANES).transpose(1, 2, 0, 3).reshape(B, H, Lp)
        o_a = _attn_prompt(q.reshape(B, Lp, A), kb.reshape(B, Lp, A), vb.reshape(B, Lp, A), fk, H=H, Dh=Dh)
        x_p = _merge(o_a.reshape(B * Lp, A), cy, sga, sgc, x_p, wa, wc, wo)
        outs["kp"].append(k.reshape(B, Lp, H, Dh)[:, :L])
        outs["vp"].append(v.reshape(B, Lp, H, Dh)[:, :L])
        outs["lfp"].append(lf.reshape(B, Lp, H)[:, :L])
        outs["cp"].append(cst)

        st = state_conv[l]
        halos = [jnp.pad(st[:, halo - s:], ((0, 0), (0, T - s), (0, 0))).reshape(NB * T, C) for s in range(1, halo + 1)]
        q, k, v, kb, vb, lf, cy, sga, sgc, u = _inproj(
            x_s, g_l, w_l, bf_l, conv_w[l], A=A, C=C, H=H, scale=scale, T=T, halos=halos)
        qbd = (q.reshape(NB, T, H, 1, Dh) * eye_h.astype(BF16)[None, None, :, :, None]).reshape(NB, T * H, A)
        pad_keys = ((0, 0), (0, P - T), (0, 0))
        o_bd = _attn_sample(
            page_table, qbd, jnp.pad(kb.reshape(NB, T, A), pad_keys), jnp.pad(vb.reshape(NB, T, A), pad_keys),
            jnp.pad(lf.reshape(NB, T, H).transpose(0, 2, 1), ((0, 0), (0, 0), (0, P - T))),
            cache_k[l].reshape(n_pool, P, A), cache_v[l].reshape(n_pool, P, A),
            cache_logf[l].transpose(0, 2, 1), T=T, H=H)
        o_a = jnp.sum(o_bd.reshape(NB, T, H, H, Dh) * eye_h[None, None, :, :, None], axis=2).reshape(NB * T, A)
        x_s = _merge(o_a.astype(BF16), cy, sga, sgc, x_s, wa, wc, wo)
        outs["ks"].append(k.reshape(NB, T, H, Dh))
        outs["vs"].append(v.reshape(NB, T, H, Dh))
        outs["lfs"].append(lf.reshape(NB, T, H))
        outs["cs"].append(u.reshape(NB, T, C)[:, T - halo:])

        gf = row(norm_ffn[l])
        gfin = row(norm_final) if last else None
        j = l // 2
        if l % 2 == 0:
            wg, wu, wd = ffn_w_gate[j].astype(BF16), ffn_w_up[j].astype(BF16), ffn_w_down[j].astype(BF16)
            x_p = _ffn(x_p, gf, wg, wu, wd, gfin)
            x_s = _ffn(x_s, gf, wg, wu, wd, gfin)
        else:
            wr = jnp.pad(moe_router[j], ((0, 0), (0, LANES - E)))
            wr_hi = wr.astype(BF16)
            wr_lo = (wr - wr_hi.astype(F32)).astype(BF16)
            wg, wu, wd = moe_w_gate[j].astype(BF16), moe_w_up[j].astype(BF16), moe_w_down[j].astype(BF16)
            x_p = _moe_dense(x_p, gf, wr_hi, wr_lo, wg, wu, wd, gfin)
            x_s = _moe_dense(x_s, gf, wr_hi, wr_lo, wg, wu, wd, gfin)

    y_prompt = x_p.reshape(B, Lp, D)[:, n_meta:L]
    y_sample = x_s.reshape(NB, T, D)
    return (y_prompt, y_sample, jnp.stack(outs["kp"]), jnp.stack(outs["vp"]), jnp.stack(outs["lfp"]),
            jnp.stack(outs["cp"]), jnp.stack(outs["ks"]), jnp.stack(outs["vs"]), jnp.stack(outs["lfs"]),
            jnp.stack(outs["cs"]))
```

```python
import functools

import jax
import jax.numpy as jnp
from jax import lax
from jax.experimental import pallas as pl
from jax.experimental.pallas import tpu as pltpu

RMS_EPS = 1e-6
TOP_K = 2
NEG_BIG = -1e30
LOG2E = 1.4426950408889634
LANES = 128
V7X_VMEM_LIMIT_BYTES = 56 * 1024 * 1024

F32 = jnp.float32
BF16 = jnp.bfloat16


def _pick_tile(n, target, mult=8):
    best = None
    for t in range(mult, min(n, target) + 1, mult):
        if n % t == 0:
            best = t
    assert best is not None, (n, target, mult)
    return best


def _params(sem, vmem=V7X_VMEM_LIMIT_BYTES):
    return pltpu.CompilerParams(dimension_semantics=sem, vmem_limit_bytes=vmem)


def _rms(x, g):
    ms = jnp.mean(x * x, axis=-1, keepdims=True)
    return (x * lax.rsqrt(ms + RMS_EPS)) * g


def _log_sigmoid(x):
    return jnp.minimum(x, 0.0) - jnp.log1p(jnp.exp(-jnp.abs(x)))


def _split3(x):
    hi = x.astype(BF16)
    r1 = x - hi.astype(F32)
    mid = r1.astype(BF16)
    lo = (r1 - mid.astype(F32)).astype(BF16)
    return hi, mid, lo


def _dot(a, b):
    return jnp.dot(a, b, preferred_element_type=F32)


def _dot3(x, w_bf16):
    hi, mid, lo = _split3(x)
    return _dot(hi, w_bf16) + _dot(mid, w_bf16) + _dot(lo, w_bf16)


def _inproj_kernel(*refs, tm, A, C, D, H, scale, halo, sample, n_aliased, tiles_per_seq, T, cst_tile, cst_row):
    if sample:
        (x_ref, g_ref, w_ref, bf_ref, cw_ref, *halo_refs) = refs[: 5 + halo]
        (q_ref, k_ref, v_ref, kb_ref, vb_ref, lf_ref, cy_ref, sga_ref, sgc_ref, u_ref) = refs[5 + halo:]
    else:
        (x_ref, g_ref, w_ref, bf_ref, cw_ref) = refs[:5]
        (q_ref, k_ref, v_ref, kb_ref, vb_ref, lf_ref, cy_ref, sga_ref, sgc_ref, cst_ref,
         carry_scr) = refs[5 + n_aliased:]

    xb = _rms(x_ref[...], g_ref[...]).astype(BF16)

    def mm(lo, hi):
        return _dot(xb, w_ref[:, lo:hi])

    zq = mm(0, 3 * A)
    q_ref[...] = (zq[:, :A] * scale).astype(BF16)
    k = zq[:, A:2 * A]
    v = zq[:, 2 * A:3 * A]
    if sample:
        k_ref[...] = k
        v_ref[...] = v
    else:
        k_ref[0, 0] = k.reshape(tm, H, A // H)
        v_ref[0, 0] = v.reshape(tm, H, A // H)
    kb_ref[...] = k.astype(BF16)
    vb_ref[...] = v.astype(BF16)

    o = 3 * A
    zc = mm(o, o + 3 * C)
    xin, bg, cg = zc[:, :C], zc[:, C:2 * C], zc[:, 2 * C:]
    u = cg * xin
    row = lax.broadcasted_iota(jnp.int32, (tm, C), 0)
    shifted = [u]
    if sample:
        pos = row % T
        for s in range(1, halo + 1):
            shifted.append(jnp.where(pos >= s, pltpu.roll(u, s, 0), halo_refs[s - 1][...]))
        u_ref[...] = u
    else:
        t = pl.program_id(0) % tiles_per_seq

        @pl.when(t == 0)
        def _():
            carry_scr[...] = jnp.zeros_like(carry_scr)

        for s in range(1, halo + 1):
            us = pltpu.roll(u, s, 0)
            for r in range(s):
                us = jnp.where(row == r, carry_scr[halo - s + r:halo - s + r + 1, :], us)
            shifted.append(us)
    y = cw_ref[0:1, :] * shifted[halo]
    for i in range(1, halo + 1):
        y = y + cw_ref[i:i + 1, :] * shifted[halo - i]
    cy_ref[...] = (bg * y).astype(BF16)
    if not sample:
        carry_scr[0:halo, :] = u[tm - halo:tm, :]

        @pl.when(t == cst_tile)
        def _():
            cst_ref[0] = u[cst_row:cst_row + halo, :]

    o += 3 * C
    zg = mm(o, o + 2 * D)
    sga_ref[...] = jax.nn.sigmoid(zg[:, :D])
    sgc_ref[...] = jax.nn.sigmoid(zg[:, D:])

    o += 2 * D
    zf = mm(o, o + LANES)
    lf_ref[...] = _log_sigmoid(zf + bf_ref[...])[:, :H]


def _inproj(x, g, w, bf, cw, *, A, C, H, scale, seq_len=None, valid_len=None, T=None, halos=None, layer=0, depth=1,
            kv_prev=(), tm_target=384):
    M, D = x.shape
    halo = cw.shape[0] - 1
    sample = halos is not None
    if sample:
        tm = M
        tiles_per_seq, cst_tile, cst_row = 1, 0, 0
        assert T >= halo
    else:
        tm = _pick_tile(seq_len, tm_target, 16)
        tiles_per_seq = seq_len // tm
        cst_tile, cst_row = divmod(valid_len - halo, tm)
        assert cst_row + halo <= tm
    n_seq = M // (tiles_per_seq * tm)
    Nw = w.shape[1]

    def row_spec(n):
        return pl.BlockSpec((tm, n), lambda i: (i, 0))

    def full_spec(a):
        return pl.BlockSpec(a.shape, lambda i: (0,) * a.ndim)

    in_specs = [row_spec(D), full_spec(g), pl.BlockSpec((D, Nw), lambda i: (0, 0), pipeline_mode=pl.Buffered(1)),
                full_spec(bf), full_spec(cw)]
    args = [x, g, w, bf, cw]
    if sample:
        in_specs += [row_spec(C) for _ in halos]
        args += list(halos)
    if sample:
        kv_shape = jax.ShapeDtypeStruct((M, A), F32)
        kv_spec = row_spec(A)
    else:
        kv_shape = jax.ShapeDtypeStruct((depth, n_seq, seq_len, H, A // H), F32)
        kv_spec = pl.BlockSpec((1, 1, tm, H, A // H),
                               lambda i: (layer, i // tiles_per_seq, i % tiles_per_seq, 0, 0))
        in_specs += [pl.BlockSpec(memory_space=pl.ANY) for _ in kv_prev]
        args += list(kv_prev)
    out_shape = [jax.ShapeDtypeStruct((M, A), BF16), kv_shape, kv_shape,
                 jax.ShapeDtypeStruct((M, A), BF16), jax.ShapeDtypeStruct((M, A), BF16), jax.ShapeDtypeStruct((M, H), F32),
                 jax.ShapeDtypeStruct((M, C), BF16), jax.ShapeDtypeStruct((M, D), F32), jax.ShapeDtypeStruct((M, D), F32)]
    out_specs = [row_spec(A), kv_spec, kv_spec, row_spec(A), row_spec(A), row_spec(H), row_spec(C), row_spec(D),
                 row_spec(D)]
    scratch = []
    if sample:
        out_shape.append(jax.ShapeDtypeStruct((M, C), F32))
        out_specs.append(row_spec(C))
    else:
        out_shape.append(jax.ShapeDtypeStruct((n_seq, halo, C), F32))
        out_specs.append(pl.BlockSpec((1, halo, C), lambda i: (i // tiles_per_seq, 0, 0)))
        scratch.append(pltpu.VMEM((8, C), F32))
    kern = functools.partial(_inproj_kernel, tm=tm, A=A, C=C, D=D, H=H, scale=scale, halo=halo, sample=sample,
                             n_aliased=len(kv_prev), tiles_per_seq=tiles_per_seq, T=T, cst_tile=cst_tile,
                             cst_row=cst_row)
    n_in = len(args)
    aliases = {n_in - len(kv_prev) + a: 1 + a for a in range(len(kv_prev))}
    return pl.pallas_call(
        kern, grid=(M // tm,), in_specs=in_specs, out_specs=out_specs, out_shape=out_shape,
        scratch_shapes=scratch, input_output_aliases=aliases, compiler_params=_params(("arbitrary",)),
        name="inproj_sample" if sample else "inproj_prompt")(*args)


def _tri_and_ones():
    i = lax.broadcasted_iota(jnp.int32, (LANES, LANES), 0)
    j = lax.broadcasted_iota(jnp.int32, (LANES, LANES), 1)
    return jnp.where(i <= j, 1.0, 0.0).astype(BF16), jnp.ones((LANES, LANES), BF16)


def _cumsum_kernel(x_ref, o_ref, *, n_chunks, R):
    tri, ones = _tri_and_ones()
    x = x_ref[...].reshape(n_chunks * R, LANES)
    hi, mid, lo = _split3(x)
    y = _dot(hi, tri) + _dot(mid, tri) + _dot(lo, tri)
    tot = _dot(hi, ones) + _dot(mid, ones) + _dot(lo, ones)
    carry = jnp.zeros((R, LANES), F32)
    for c in range(n_chunks):
        o_ref[c] = (y[c * R:(c + 1) * R] + carry) * LOG2E
        carry = carry + tot[c * R:(c + 1) * R]


def _cumsum_chunks(x):
    n_chunks, R, _ = x.shape
    return pl.pallas_call(
        functools.partial(_cumsum_kernel, n_chunks=n_chunks, R=R),
        out_shape=jax.ShapeDtypeStruct(x.shape, F32), name="logf_cumsum")(x)


def _attn_kernel(q_ref, k_ref, v_ref, fk_ref, o_ref, qm_scr, mrun_scr, m_scr, lrun_scr, acc_scr, *, tq, tk, H, Dh):
    qi = pl.program_id(1)
    r = tq // tk
    n_full = qi * r
    n_sub = tk // LANES
    G = LANES // Dh
    contract_last = (((1,), (1,)), ((), ()))
    lane = lax.broadcasted_iota(jnp.int32, (tq, LANES), 1)

    def group(h):
        return slice((h // G) * LANES, (h // G + 1) * LANES)

    def own_lanes(h):
        return (lane >= (h % G) * Dh) & (lane < (h % G + 1) * Dh)

    for h in range(H):
        qg = q_ref[0, :, group(h)]
        qm_scr[h] = jnp.where(own_lanes(h), qg, jnp.zeros_like(qg))

    def scores(h, j, masked):
        start = pl.multiple_of(j * tk, tk)
        s = lax.dot_general(qm_scr[h], k_ref[0, pl.ds(start, tk), group(h)], contract_last,
                            preferred_element_type=F32)
        s = s - fk_ref[0, h:h + 1, pl.ds(start, tk)]
        if masked:
            qpos = qi * tq + lax.broadcasted_iota(jnp.int32, (tq, tk), 0)
            kpos = start + lax.broadcasted_iota(jnp.int32, (tq, tk), 1)
            s = jnp.where(kpos <= qpos, s, NEG_BIG)
        return [s[:, c * LANES:(c + 1) * LANES] for c in range(n_sub)], start

    def max_pass(j, masked):
        for h in range(H):
            parts, _ = scores(h, j, masked)
            mrun_scr[h] = functools.reduce(jnp.maximum, parts, mrun_scr[h])

    def sum_pass(j, masked):
        for h in range(H):
            parts, start = scores(h, j, masked)
            m = m_scr[h]
            probs = [jnp.exp2(part - m) for part in parts]
            lrun_scr[h] = functools.reduce(jnp.add, probs, lrun_scr[h])
            p = jnp.concatenate(probs, axis=1).astype(BF16)
            acc_scr[h] += _dot(p, v_ref[0, pl.ds(start, tk), group(h)])

    def run(step):
        def body(j, carry):
            step(j, False)
            return carry

        lax.fori_loop(0, n_full, body, 0)
        for d in range(r):
            step(n_full + d, True)

    mrun_scr[...] = jnp.full_like(mrun_scr, NEG_BIG)
    run(max_pass)
    for h in range(H):
        m_scr[h] = jnp.broadcast_to(jnp.max(mrun_scr[h], axis=1, keepdims=True), (tq, LANES))
    lrun_scr[...] = jnp.zeros_like(lrun_scr)
    acc_scr[...] = jnp.zeros_like(acc_scr)
    run(sum_pass)
    for g in range(H // G):
        out = jnp.zeros((tq, LANES), F32)
        for h in range(g * G, (g + 1) * G):
            l = jnp.sum(lrun_scr[h], axis=1, keepdims=True)
            out = jnp.where(own_lanes(h), acc_scr[h] / l, out)
        o_ref[0, :, group(g * G)] = out.astype(BF16)


def _attn_prompt(q, k, v, fk, *, H, Dh, tq=256, tk=256):
    B, Lp, A = q.shape
    assert LANES % Dh == 0 and H % (LANES // Dh) == 0
    kv_spec = pl.BlockSpec((1, Lp, A), lambda b, i: (b, 0, 0), pipeline_mode=pl.Buffered(1))
    stat = pltpu.VMEM((H, tq, LANES), F32)
    return pl.pallas_call(
        functools.partial(_attn_kernel, tq=tq, tk=tk, H=H, Dh=Dh),
        grid=(B, Lp // tq),
        in_specs=[pl.BlockSpec((1, tq, A), lambda b, i: (b, i, 0)), kv_spec, kv_spec,
                  pl.BlockSpec((1, H, Lp), lambda b, i: (b, 0, 0))],
        out_specs=pl.BlockSpec((1, tq, A), lambda b, i: (b, i, 0)),
        out_shape=jax.ShapeDtypeStruct((B, Lp, A), BF16),
        scratch_shapes=[pltpu.VMEM((H, tq, LANES), BF16), stat, stat, stat, stat],
        compiler_params=_params(("arbitrary", "arbitrary")), name="attn_prompt")(q, k, v, fk)


def _attn_sample_kernel(pt_ref, qbd_ref, knew_ref, vnew_ref, lfnew_ref, *refs, PP, T, H, P):
    k_refs, v_refs, lf_refs = refs[:PP], refs[PP:2 * PP], refs[2 * PP:3 * PP]
    o_ref, m_scr, l_scr, acc_scr, carry_scr = refs[3 * PP:]
    j = pl.program_id(1)
    R = T * H
    contract_last = (((1,), (1,)), ((), ()))

    @pl.when(j == 0)
    def _():
        m_scr[...] = jnp.full_like(m_scr, NEG_BIG)
        l_scr[...] = jnp.zeros_like(l_scr)
        acc_scr[...] = jnp.zeros_like(acc_scr)
        carry_scr[...] = jnp.zeros_like(carry_scr)

    tri_ones = jnp.concatenate(_tri_and_ones(), axis=1)
    qbd = qbd_ref[0]

    def page(state, lf, kb, vb, mask):
        m, l, acc, carry = state
        pad = (-H) % 16
        lf_p = jnp.concatenate([lf, jnp.zeros((pad, P), F32)], axis=0) if pad else lf
        y = _dot(jnp.concatenate(_split3(lf_p), axis=0), tri_ones)
        y = y[0:H] + y[H + pad:2 * H + pad] + y[2 * (H + pad):3 * H + 2 * pad]
        fk = carry + y[:, :P]
        carry = carry + y[:, P:]
        s = lax.dot_general(qbd, kb, contract_last, preferred_element_type=F32)
        s = s - jnp.concatenate([fk * LOG2E] * T, axis=0)
        if mask is not None:
            s = jnp.where(mask, s, NEG_BIG)
        m_new = jnp.maximum(m, jnp.max(s, axis=1, keepdims=True))
        alpha = jnp.exp2(m - m_new)
        p = jnp.exp2(s - m_new)
        l = alpha * l + jnp.sum(p, axis=1, keepdims=True)
        acc = alpha * acc + _dot(p.astype(BF16), vb)
        return m_new, l, acc, carry

    state = (m_scr[...], l_scr[...], acc_scr[...], carry_scr[...])
    for p in range(PP):
        state = page(state, lf_refs[p][0], k_refs[p][0].astype(BF16), v_refs[p][0].astype(BF16), None)
    m_scr[...], l_scr[...], acc_scr[...], carry_scr[...] = state

    @pl.when(j == pl.num_programs(1) - 1)
    def _():
        t_row = lax.broadcasted_iota(jnp.int32, (R, P), 0) // H
        key = lax.broadcasted_iota(jnp.int32, (R, P), 1)
        _, l, acc, _ = page(state, lfnew_ref[0], knew_ref[0], vnew_ref[0], key <= t_row)
        o_ref[0] = acc / l


def _attn_sample(page_table, qbd, k_new, v_new, lf_new, cache_k, cache_v, cache_lf, *, T, H, pages_per_step=8):
    n_seq, n_pages = page_table.shape
    _, P, A = cache_k.shape
    R = T * H
    PP = _pick_tile(n_pages, pages_per_step, 1)

    def seq_spec(shape):
        return pl.BlockSpec((1,) + shape, lambda s, j, pt: (s, 0, 0))

    def page_spec(shape, p):
        return pl.BlockSpec((1,) + shape, lambda s, j, pt: (pt[s, j * PP + p], 0, 0))

    in_specs = ([seq_spec((R, A)), seq_spec((P, A)), seq_spec((P, A)), seq_spec((H, P))]
                + [page_spec((P, A), p) for p in range(PP)] * 2
                + [page_spec((H, P), p) for p in range(PP)])
    grid_spec = pltpu.PrefetchScalarGridSpec(
        num_scalar_prefetch=1, grid=(n_seq, n_pages // PP), in_specs=in_specs,
        out_specs=pl.BlockSpec((1, R, A), lambda s, j, pt: (s, 0, 0)),
        scratch_shapes=[pltpu.VMEM((R, 1), F32), pltpu.VMEM((R, 1), F32), pltpu.VMEM((R, A), F32),
                        pltpu.VMEM((H, P), F32)])
    return pl.pallas_call(
        functools.partial(_attn_sample_kernel, PP=PP, T=T, H=H, P=P), grid_spec=grid_spec,
        out_shape=jax.ShapeDtypeStruct((n_seq, R, A), F32),
        compiler_params=_params(("arbitrary", "arbitrary")), name="attn_sample")(
            page_table, qbd, k_new, v_new, lf_new, *([cache_k] * PP), *([cache_v] * PP), *([cache_lf] * PP))


def _merge_kernel(o_ref, cy_ref, sga_ref, sgc_ref, x_ref, wa_ref, wc_ref, wo_ref, out_ref):
    ya = _dot(o_ref[...], wa_ref[...])
    yc = _dot(cy_ref[...], wc_ref[...])
    mix = (sga_ref[...] * ya + sgc_ref[...] * yc).astype(BF16)
    out_ref[...] = x_ref[...] + _dot(mix, wo_ref[...])


def _merge(o, cy, sga, sgc, x, wa, wc, wo, *, tm_target=512):
    M, D = x.shape
    tm = _pick_tile(M, tm_target, 16)

    def row_spec(n):
        return pl.BlockSpec((tm, n), lambda i: (i, 0))

    def w_spec(a):
        return pl.BlockSpec(a.shape, lambda i: (0, 0))

    return pl.pallas_call(
        _merge_kernel, grid=(M // tm,),
        in_specs=[row_spec(o.shape[1]), row_spec(cy.shape[1]), row_spec(D), row_spec(D), row_spec(D),
                  w_spec(wa), w_spec(wc), w_spec(wo)],
        out_specs=row_spec(D), out_shape=jax.ShapeDtypeStruct((M, D), F32),
        compiler_params=_params(("arbitrary",)), name="merge")(o, cy, sga, sgc, x, wa, wc, wo)


def _swiglu_step(xb, wg, wu, wd, row_scale=None):
    gate = _dot(xb, wg)
    up = _dot(xb, wu)
    h = jax.nn.silu(gate) * up
    if row_scale is not None:
        h = h * row_scale
    return _dot(h.astype(BF16), wd)


def _finish(x, acc, gfin_ref, out_ref):
    y = x + acc
    if gfin_ref is not None:
        y = _rms(y, gfin_ref[...])
    out_ref[...] = y


def _ffn_kernel(*refs, final):
    if final:
        x_ref, g_ref, wg_ref, wu_ref, wd_ref, gfin_ref, out_ref, xn_scr, acc_scr = refs
    else:
        x_ref, g_ref, wg_ref, wu_ref, wd_ref, out_ref, xn_scr, acc_scr = refs
        gfin_ref = None
    f = pl.program_id(1)

    @pl.when(f == 0)
    def _():
        xn_scr[...] = _rms(x_ref[...], g_ref[...]).astype(BF16)
        acc_scr[...] = jnp.zeros_like(acc_scr)

    acc_scr[...] += _swiglu_step(xn_scr[...], wg_ref[...], wu_ref[...], wd_ref[...])

    @pl.when(f == pl.num_programs(1) - 1)
    def _():
        _finish(x_ref[...], acc_scr[...], gfin_ref, out_ref)


def _ffn(x, g, wg, wu, wd, gfin=None, *, tm_target=1024, tf_target=512):
    M, D = x.shape
    F = wg.shape[1]
    tm = _pick_tile(M, tm_target, 16)
    tf = _pick_tile(F, tf_target, LANES)
    final = gfin is not None
    in_specs = [pl.BlockSpec((tm, D), lambda i, f: (i, 0)), pl.BlockSpec((1, D), lambda i, f: (0, 0)),
                pl.BlockSpec((D, tf), lambda i, f: (0, f)), pl.BlockSpec((D, tf), lambda i, f: (0, f)),
                pl.BlockSpec((tf, D), lambda i, f: (f, 0))]
    args = [x, g, wg, wu, wd]
    if final:
        in_specs.append(pl.BlockSpec((1, D), lambda i, f: (0, 0)))
        args.append(gfin)
    return pl.pallas_call(
        functools.partial(_ffn_kernel, final=final), grid=(M // tm, F // tf), in_specs=in_specs,
        out_specs=pl.BlockSpec((tm, D), lambda i, f: (i, 0)), out_shape=jax.ShapeDtypeStruct((M, D), F32),
        scratch_shapes=[pltpu.VMEM((tm, D), BF16), pltpu.VMEM((tm, D), F32)],
        compiler_params=_params(("arbitrary", "arbitrary")), name="ffn_dense")(*args)


def _route(xn, wr_hi, wr_lo, n_experts):
    hi = xn.astype(BF16)
    lo = (xn - hi.astype(F32)).astype(BF16)
    logits = _dot(hi, wr_hi) + _dot(lo, wr_hi) + _dot(hi, wr_lo)
    lane = lax.broadcasted_iota(jnp.int32, logits.shape, 1).astype(F32)
    lg = jnp.where(lane < n_experts, logits, NEG_BIG)
    v1 = jnp.max(lg, axis=1, keepdims=True)
    i1 = jnp.min(jnp.where(lg == v1, lane, float(LANES)), axis=1, keepdims=True)
    lg2 = jnp.where(lane == i1, NEG_BIG, lg)
    v2 = jnp.max(lg2, axis=1, keepdims=True)
    i2 = jnp.min(jnp.where(lg2 == v2, lane, float(LANES)), axis=1, keepdims=True)
    e2 = jnp.exp(v2 - v1)
    den = 1.0 + e2
    return jnp.where(lane == i1, 1.0 / den, 0.0) + jnp.where(lane == i2, e2 / den, 0.0)


def _moe_dense_kernel(*refs, final, n_experts):
    if final:
        x_ref, g_ref, wrh_ref, wrl_ref, wg_ref, wu_ref, wd_ref, gfin_ref, out_ref, xn_scr, acc_scr, comb_scr = refs
    else:
        x_ref, g_ref, wrh_ref, wrl_ref, wg_ref, wu_ref, wd_ref, out_ref, xn_scr, acc_scr, comb_scr = refs
        gfin_ref = None
    e = pl.program_id(1)
    f = pl.program_id(2)

    @pl.when((e == 0) & (f == 0))
    def _():
        xn = _rms(x_ref[...], g_ref[...])
        xn_scr[...] = xn.astype(BF16)
        comb_scr[...] = _route(xn, wrh_ref[...], wrl_ref[...], n_experts)
        acc_scr[...] = jnp.zeros_like(acc_scr)

    comb = comb_scr[...]
    lane = lax.broadcasted_iota(jnp.int32, comb.shape, 1)
    ce = jnp.sum(jnp.where(lane == e, comb, 0.0), axis=1, keepdims=True)
    acc_scr[...] += _swiglu_step(xn_scr[...], wg_ref[0], wu_ref[0], wd_ref[0], row_scale=ce)

    @pl.when((e == pl.num_programs(1) - 1) & (f == pl.num_programs(2) - 1))
    def _():
        _finish(x_ref[...], acc_scr[...], gfin_ref, out_ref)


def _moe_dense(x, g, wr_hi, wr_lo, wg, wu, wd, gfin=None, *, tm_target=1024, tf_target=512):
    M, D = x.shape
    E, _, F = wg.shape
    tm = _pick_tile(M, tm_target, 16)
    tf = _pick_tile(F, tf_target, LANES)
    final = gfin is not None
    vec = pl.BlockSpec((1, D), lambda i, e, f: (0, 0))
    wr_spec = pl.BlockSpec((D, LANES), lambda i, e, f: (0, 0))
    in_specs = [pl.BlockSpec((tm, D), lambda i, e, f: (i, 0)), vec, wr_spec, wr_spec,
                pl.BlockSpec((1, D, tf), lambda i, e, f: (e, 0, f)), pl.BlockSpec((1, D, tf), lambda i, e, f: (e, 0, f)),
                pl.BlockSpec((1, tf, D), lambda i, e, f: (e, f, 0))]
    args = [x, g, wr_hi, wr_lo, wg, wu, wd]
    if final:
        in_specs.append(vec)
        args.append(gfin)
    return pl.pallas_call(
        functools.partial(_moe_dense_kernel, final=final, n_experts=E), grid=(M // tm, E, F // tf),
        in_specs=in_specs, out_specs=pl.BlockSpec((tm, D), lambda i, e, f: (i, 0)),
        out_shape=jax.ShapeDtypeStruct((M, D), F32),
        scratch_shapes=[pltpu.VMEM((tm, D), BF16), pltpu.VMEM((tm, D), F32), pltpu.VMEM((tm, LANES), F32)],
        compiler_params=_params(("arbitrary", "arbitrary", "arbitrary")), name="moe_dense")(*args)


def kernel(x_prompt, x_sample, cache_k, cache_v, cache_logf, state_conv, page_table, meta_tokens, norm_mix, w_in,
           b_forget, conv_w, w_out_attn, w_out_conv, w_o, norm_ffn, ffn_w_gate, ffn_w_up, ffn_w_down, moe_router,
           moe_w_gate, moe_w_up, moe_w_down, norm_final):
    B, S, D = x_prompt.shape
    NB, T, _ = x_sample.shape
    depth, n_pool, P, H, Dh = cache_k.shape
    n_meta = meta_tokens.shape[0]
    A = H * Dh
    C = conv_w.shape[2]
    halo = conv_w.shape[1] - 1
    E = moe_router.shape[2]
    L = S + n_meta
    Lp = -(-L // 256) * 256
    n_chunks = Lp // LANES
    scale = float(Dh) ** -0.5 * LOG2E

    o_f = 3 * A
    o_c = o_f + H

    def prep_w_in(w):
        wf = jnp.pad(w[:, o_f:o_c], ((0, 0), (0, LANES - H)))
        return jnp.concatenate([w[:, :o_f], w[:, o_c:], wf], axis=1).astype(BF16)

    def row(v):
        return v.reshape(1, -1)

    meta = jnp.broadcast_to(meta_tokens[None].astype(x_prompt.dtype), (B, n_meta, D))
    x_p = jnp.concatenate([meta, x_prompt, jnp.zeros((B, Lp - L, D), x_prompt.dtype)], axis=1).reshape(B * Lp, D)
    x_s = x_sample.reshape(NB * T, D)
    eye_h = jnp.eye(H, dtype=F32)

    outs = {n: [] for n in ("lfp", "cp", "ks", "vs", "lfs", "cs")}
    kv_prev = ()
    for l in range(depth):
        last = l == depth - 1
        w_l = prep_w_in(w_in[l])
        bf_l = jnp.pad(b_forget[l], (0, LANES - H)).reshape(1, LANES)
        g_l = row(norm_mix[l])
        wa, wc, wo = w_out_attn[l].astype(BF16), w_out_conv[l].astype(BF16), w_o[l].astype(BF16)

        q, k_all, v_all, kb, vb, lf, cy, sga, sgc, cst = _inproj(
            x_p, g_l, w_l, bf_l, conv_w[l], A=A, C=C, H=H, scale=scale, seq_len=Lp, valid_len=L, layer=l,
            depth=depth, kv_prev=kv_prev)
        kv_prev = (k_all, v_all)
        lf_chunks = lf.reshape(B, n_chunks, LANES, H).transpose(1, 0, 3, 2).reshape(n_chunks, B * H, LANES)
        fk = _cumsum_chunks(lf_chunks).reshape(n_chunks, B, H, LANES).transpose(1, 2, 0, 3).reshape(B, H, Lp)
        o_a = _attn_prompt(q.reshape(B, Lp, A), kb.reshape(B, Lp, A), vb.reshape(B, Lp, A), fk, H=H, Dh=Dh)
        x_p = _merge(o_a.reshape(B * Lp, A), cy, sga, sgc, x_p, wa, wc, wo)
        outs["lfp"].append(lf.reshape(B, Lp, H)[:, :L])
        outs["cp"].append(cst)

        st = state_conv[l]
        halos = [jnp.pad(st[:, halo - s:], ((0, 0), (0, T - s), (0, 0))).reshape(NB * T, C) for s in range(1, halo + 1)]
        q, k, v, kb, vb, lf, cy, sga, sgc, u = _inproj(
            x_s, g_l, w_l, bf_l, conv_w[l], A=A, C=C, H=H, scale=scale, T=T, halos=halos)
        qbd = (q.reshape(NB, T, H, 1, Dh) * eye_h.astype(BF16)[None, None, :, :, None]).reshape(NB, T * H, A)
        pad_keys = ((0, 0), (0, P - T), (0, 0))
        o_bd = _attn_sample(
            page_table, qbd, jnp.pad(kb.reshape(NB, T, A), pad_keys), jnp.pad(vb.reshape(NB, T, A), pad_keys),
            jnp.pad(lf.reshape(NB, T, H).transpose(0, 2, 1), ((0, 0), (0, 0), (0, P - T))),
            cache_k[l].reshape(n_pool, P, A), cache_v[l].reshape(n_pool, P, A),
            cache_logf[l].transpose(0, 2, 1), T=T, H=H)
        o_a = jnp.sum(o_bd.reshape(NB, T, H, H, Dh) * eye_h[None, None, :, :, None], axis=2).reshape(NB * T, A)
        x_s = _merge(o_a.astype(BF16), cy, sga, sgc, x_s, wa, wc, wo)
        outs["ks"].append(k.reshape(NB, T, H, Dh))
        outs["vs"].append(v.reshape(NB, T, H, Dh))
        outs["lfs"].append(lf.reshape(NB, T, H))
        outs["cs"].append(u.reshape(NB, T, C)[:, T - halo:])

        gf = row(norm_ffn[l])
        gfin = row(norm_final) if last else None
        j = l // 2
        if l % 2 == 0:
            wg, wu, wd = ffn_w_gate[j].astype(BF16), ffn_w_up[j].astype(BF16), ffn_w_down[j].astype(BF16)
            x_p = _ffn(x_p, gf, wg, wu, wd, gfin)
            x_s = _ffn(x_s, gf, wg, wu, wd, gfin)
        else:
            wr = jnp.pad(moe_router[j], ((0, 0), (0, LANES - E)))
            wr_hi = wr.astype(BF16)
            wr_lo = (wr - wr_hi.astype(F32)).astype(BF16)
            wg, wu, wd = moe_w_gate[j].astype(BF16), moe_w_up[j].astype(BF16), moe_w_down[j].astype(BF16)
            x_p = _moe_dense(x_p, gf, wr_hi, wr_lo, wg, wu, wd, gfin)
            x_s = _moe_dense(x_s, gf, wr_hi, wr_lo, wg, wu, wd, gfin)

    y_prompt = x_p.reshape(B, Lp, D)[:, n_meta:L]
    y_sample = x_s.reshape(NB, T, D)
    return (y_prompt, y_sample, kv_prev[0][:, :, :L], kv_prev[1][:, :, :L], jnp.stack(outs["lfp"]),
            jnp.stack(outs["cp"]), jnp.stack(outs["ks"]), jnp.stack(outs["vs"]), jnp.stack(outs["lfs"]),
            jnp.stack(outs["cs"]))
```

```python
import functools

import jax
import jax.numpy as jnp
from jax import lax
from jax.experimental import pallas as pl
from jax.experimental.pallas import tpu as pltpu

RMS_EPS = 1e-6
TOP_K = 2
NEG_BIG = -1e30
LOG2E = 1.4426950408889634
LANES = 128
V7X_VMEM_LIMIT_BYTES = 56 * 1024 * 1024

F32 = jnp.float32
BF16 = jnp.bfloat16


def _pick_tile(n, target, mult=8):
    best = None
    for t in range(mult, min(n, target) + 1, mult):
        if n % t == 0:
            best = t
    assert best is not None, (n, target, mult)
    return best


def _params(sem, vmem=V7X_VMEM_LIMIT_BYTES):
    return pltpu.CompilerParams(dimension_semantics=sem, vmem_limit_bytes=vmem)


def _rms(x, g):
    ms = jnp.mean(x * x, axis=-1, keepdims=True)
    return (x * lax.rsqrt(ms + RMS_EPS)) * g


def _log_sigmoid(x):
    return jnp.minimum(x, 0.0) - jnp.log1p(jnp.exp(-jnp.abs(x)))


def _split3(x):
    hi = x.astype(BF16)
    r1 = x - hi.astype(F32)
    mid = r1.astype(BF16)
    lo = (r1 - mid.astype(F32)).astype(BF16)
    return hi, mid, lo


def _dot(a, b):
    return jnp.dot(a, b, preferred_element_type=F32)


def _dot3(x, w_bf16):
    hi, mid, lo = _split3(x)
    return _dot(hi, w_bf16) + _dot(mid, w_bf16) + _dot(lo, w_bf16)


def _inproj_kernel(*refs, tm, A, C, D, H, scale, halo, sample, n_aliased, tiles_per_seq, T, cst_tile, cst_row):
    if sample:
        (x_ref, g_ref, w_ref, bf_ref, cw_ref, *halo_refs) = refs[: 5 + halo]
        (q_ref, k_ref, v_ref, kb_ref, vb_ref, lf_ref, cy_ref, sga_ref, sgc_ref, u_ref) = refs[5 + halo:]
    else:
        (x_ref, g_ref, w_ref, bf_ref, cw_ref) = refs[:5]
        (q_ref, k_ref, v_ref, kb_ref, vb_ref, lf_ref, cy_ref, sga_ref, sgc_ref, cst_ref,
         carry_scr) = refs[5 + n_aliased:]

    xb = _rms(x_ref[...], g_ref[...]).astype(BF16)

    def mm(lo, hi):
        return _dot(xb, w_ref[:, lo:hi])

    zq = mm(0, 3 * A)
    q_ref[...] = (zq[:, :A] * scale).astype(BF16)
    k = zq[:, A:2 * A]
    v = zq[:, 2 * A:3 * A]
    if sample:
        k_ref[...] = k
        v_ref[...] = v
    else:
        k_ref[0, 0] = k.reshape(tm, H, A // H)
        v_ref[0, 0] = v.reshape(tm, H, A // H)
    kb_ref[...] = k.astype(BF16)
    vb_ref[...] = v.astype(BF16)

    o = 3 * A
    zc = mm(o, o + 3 * C)
    xin, bg, cg = zc[:, :C], zc[:, C:2 * C], zc[:, 2 * C:]
    u = cg * xin
    row = lax.broadcasted_iota(jnp.int32, (tm, C), 0)
    shifted = [u]
    if sample:
        pos = row % T
        for s in range(1, halo + 1):
            shifted.append(jnp.where(pos >= s, pltpu.roll(u, s, 0), halo_refs[s - 1][...]))
        u_ref[...] = u
    else:
        t = pl.program_id(0) % tiles_per_seq

        @pl.when(t == 0)
        def _():
            carry_scr[...] = jnp.zeros_like(carry_scr)

        for s in range(1, halo + 1):
            us = pltpu.roll(u, s, 0)
            for r in range(s):
                us = jnp.where(row == r, carry_scr[halo - s + r:halo - s + r + 1, :], us)
            shifted.append(us)
    y = cw_ref[0:1, :] * shifted[halo]
    for i in range(1, halo + 1):
        y = y + cw_ref[i:i + 1, :] * shifted[halo - i]
    cy_ref[...] = (bg * y).astype(BF16)
    if not sample:
        carry_scr[0:halo, :] = u[tm - halo:tm, :]

        @pl.when(t == cst_tile)
        def _():
            cst_ref[0] = u[cst_row:cst_row + halo, :]

    o += 3 * C
    zg = mm(o, o + 2 * D)
    sga_ref[...] = jax.nn.sigmoid(zg[:, :D])
    sgc_ref[...] = jax.nn.sigmoid(zg[:, D:])

    o += 2 * D
    zf = mm(o, o + LANES)
    lf_ref[...] = _log_sigmoid(zf + bf_ref[...])[:, :H]


def _inproj(x, g, w, bf, cw, *, A, C, H, scale, seq_len=None, valid_len=None, T=None, halos=None, layer=0, depth=1,
            kv_prev=(), tm_target=384):
    M, D = x.shape
    halo = cw.shape[0] - 1
    sample = halos is not None
    if sample:
        tm = M
        tiles_per_seq, cst_tile, cst_row = 1, 0, 0
        assert T >= halo
    else:
        tm = _pick_tile(seq_len, tm_target, 16)
        tiles_per_seq = seq_len // tm
        cst_tile, cst_row = divmod(valid_len - halo, tm)
        assert cst_row + halo <= tm
    n_seq = M // (tiles_per_seq * tm)
    Nw = w.shape[1]

    def row_spec(n):
        return pl.BlockSpec((tm, n), lambda i: (i, 0))

    def full_spec(a):
        return pl.BlockSpec(a.shape, lambda i: (0,) * a.ndim)

    in_specs = [row_spec(D), full_spec(g), pl.BlockSpec((D, Nw), lambda i: (0, 0), pipeline_mode=pl.Buffered(1)),
                full_spec(bf), full_spec(cw)]
    args = [x, g, w, bf, cw]
    if sample:
        in_specs += [row_spec(C) for _ in halos]
        args += list(halos)
    if sample:
        kv_shape = jax.ShapeDtypeStruct((M, A), F32)
        kv_spec = row_spec(A)
    else:
        kv_shape = jax.ShapeDtypeStruct((depth, n_seq, seq_len, H, A // H), F32)
        kv_spec = pl.BlockSpec((1, 1, tm, H, A // H),
                               lambda i: (layer, i // tiles_per_seq, i % tiles_per_seq, 0, 0))
        in_specs += [pl.BlockSpec(memory_space=pl.ANY) for _ in kv_prev]
        args += list(kv_prev)
    out_shape = [jax.ShapeDtypeStruct((M, A), BF16), kv_shape, kv_shape,
                 jax.ShapeDtypeStruct((M, A), BF16), jax.ShapeDtypeStruct((M, A), BF16), jax.ShapeDtypeStruct((M, H), F32),
                 jax.ShapeDtypeStruct((M, C), BF16), jax.ShapeDtypeStruct((M, D), F32), jax.ShapeDtypeStruct((M, D), F32)]
    out_specs = [row_spec(A), kv_spec, kv_spec, row_spec(A), row_spec(A), row_spec(H), row_spec(C), row_spec(D),
                 row_spec(D)]
    scratch = []
    if sample:
        out_shape.append(jax.ShapeDtypeStruct((M, C), F32))
        out_specs.append(row_spec(C))
    else:
        out_shape.append(jax.ShapeDtypeStruct((n_seq, halo, C), F32))
        out_specs.append(pl.BlockSpec((1, halo, C), lambda i: (i // tiles_per_seq, 0, 0)))
        scratch.append(pltpu.VMEM((8, C), F32))
    kern = functools.partial(_inproj_kernel, tm=tm, A=A, C=C, D=D, H=H, scale=scale, halo=halo, sample=sample,
                             n_aliased=len(kv_prev), tiles_per_seq=tiles_per_seq, T=T, cst_tile=cst_tile,
                             cst_row=cst_row)
    n_in = len(args)
    aliases = {n_in - len(kv_prev) + a: 1 + a for a in range(len(kv_prev))}
    return pl.pallas_call(
        kern, grid=(M // tm,), in_specs=in_specs, out_specs=out_specs, out_shape=out_shape,
        scratch_shapes=scratch, input_output_aliases=aliases, compiler_params=_params(("arbitrary",)),
        name="inproj_sample" if sample else "inproj_prompt")(*args)


def _tri_and_ones():
    i = lax.broadcasted_iota(jnp.int32, (LANES, LANES), 0)
    j = lax.broadcasted_iota(jnp.int32, (LANES, LANES), 1)
    return jnp.where(i <= j, 1.0, 0.0).astype(BF16), jnp.ones((LANES, LANES), BF16)


def _cumsum_kernel(x_ref, o_ref, *, n_chunks, R):
    tri, ones = _tri_and_ones()
    x = x_ref[...].reshape(n_chunks * R, LANES)
    hi, mid, lo = _split3(x)
    y = _dot(hi, tri) + _dot(mid, tri) + _dot(lo, tri)
    tot = _dot(hi, ones) + _dot(mid, ones) + _dot(lo, ones)
    carry = jnp.zeros((R, LANES), F32)
    for c in range(n_chunks):
        o_ref[c] = (y[c * R:(c + 1) * R] + carry) * LOG2E
        carry = carry + tot[c * R:(c + 1) * R]


def _cumsum_chunks(x):
    n_chunks, R, _ = x.shape
    return pl.pallas_call(
        functools.partial(_cumsum_kernel, n_chunks=n_chunks, R=R),
        out_shape=jax.ShapeDtypeStruct(x.shape, F32), name="logf_cumsum")(x)


def _attn_kernel(q_ref, k_ref, v_ref, fk_ref, o_ref, qm_scr, mrun_scr, m_scr, lrun_scr, acc_scr, *, tq, tk, H, Dh):
    qi = pl.program_id(1)
    r = tq // tk
    n_full = qi * r
    n_sub = tk // LANES
    G = LANES // Dh
    contract_last = (((1,), (1,)), ((), ()))
    lane = lax.broadcasted_iota(jnp.int32, (tq, LANES), 1)

    def group(h):
        return slice((h // G) * LANES, (h // G + 1) * LANES)

    def own_lanes(h):
        return (lane >= (h % G) * Dh) & (lane < (h % G + 1) * Dh)

    for h in range(H):
        qg = q_ref[0, :, group(h)]
        qm_scr[h] = jnp.where(own_lanes(h), qg, jnp.zeros_like(qg))

    def scores(h, j, masked):
        start = pl.multiple_of(j * tk, tk)
        s = lax.dot_general(qm_scr[h], k_ref[0, pl.ds(start, tk), group(h)], contract_last,
                            preferred_element_type=F32)
        s = s - fk_ref[0, h:h + 1, pl.ds(start, tk)]
        if masked:
            qpos = qi * tq + lax.broadcasted_iota(jnp.int32, (tq, tk), 0)
            kpos = start + lax.broadcasted_iota(jnp.int32, (tq, tk), 1)
            s = jnp.where(kpos <= qpos, s, NEG_BIG)
        return [s[:, c * LANES:(c + 1) * LANES] for c in range(n_sub)], start

    def max_pass(j, masked):
        for h in range(H):
            parts, _ = scores(h, j, masked)
            mrun_scr[h] = functools.reduce(jnp.maximum, parts, mrun_scr[h])

    def sum_pass(j, masked):
        for h in range(H):
            parts, start = scores(h, j, masked)
            m = m_scr[h]
            probs = [jnp.exp2(part - m) for part in parts]
            lrun_scr[h] = functools.reduce(jnp.add, probs, lrun_scr[h])
            p = jnp.concatenate(probs, axis=1).astype(BF16)
            acc_scr[h] += _dot(p, v_ref[0, pl.ds(start, tk), group(h)])

    def run(step):
        def body(j, carry):
            step(j, False)
            return carry

        lax.fori_loop(0, n_full, body, 0)
        for d in range(r):
            step(n_full + d, True)

    mrun_scr[...] = jnp.full_like(mrun_scr, NEG_BIG)
    run(max_pass)
    for h in range(H):
        m_scr[h] = jnp.broadcast_to(jnp.max(mrun_scr[h], axis=1, keepdims=True), (tq, LANES))
    lrun_scr[...] = jnp.zeros_like(lrun_scr)
    acc_scr[...] = jnp.zeros_like(acc_scr)
    run(sum_pass)
    for g in range(H // G):
        out = jnp.zeros((tq, LANES), F32)
        for h in range(g * G, (g + 1) * G):
            l = jnp.sum(lrun_scr[h], axis=1, keepdims=True)
            out = jnp.where(own_lanes(h), acc_scr[h] / l, out)
        o_ref[0, :, group(g * G)] = out.astype(BF16)


def _attn_prompt(q, k, v, fk, *, H, Dh, tq=256, tk=256):
    B, Lp, A = q.shape
    assert LANES % Dh == 0 and H % (LANES // Dh) == 0
    kv_spec = pl.BlockSpec((1, Lp, A), lambda b, i: (b, 0, 0), pipeline_mode=pl.Buffered(1))
    stat = pltpu.VMEM((H, tq, LANES), F32)
    return pl.pallas_call(
        functools.partial(_attn_kernel, tq=tq, tk=tk, H=H, Dh=Dh),
        grid=(B, Lp // tq),
        in_specs=[pl.BlockSpec((1, tq, A), lambda b, i: (b, i, 0)), kv_spec, kv_spec,
                  pl.BlockSpec((1, H, Lp), lambda b, i: (b, 0, 0))],
        out_specs=pl.BlockSpec((1, tq, A), lambda b, i: (b, i, 0)),
        out_shape=jax.ShapeDtypeStruct((B, Lp, A), BF16),
        scratch_shapes=[pltpu.VMEM((H, tq, LANES), BF16), stat, stat, stat, stat],
        compiler_params=_params(("arbitrary", "arbitrary")), name="attn_prompt")(q, k, v, fk)


def _attn_sample_kernel(pt_ref, qbd_ref, knew_ref, vnew_ref, lfnew_ref, *refs, PP, T, H, P):
    k_refs, v_refs, lf_refs = refs[:PP], refs[PP:2 * PP], refs[2 * PP:3 * PP]
    o_ref, m_scr, l_scr, acc_scr, carry_scr = refs[3 * PP:]
    j = pl.program_id(1)
    R = T * H
    contract_last = (((1,), (1,)), ((), ()))

    @pl.when(j == 0)
    def _():
        m_scr[...] = jnp.full_like(m_scr, NEG_BIG)
        l_scr[...] = jnp.zeros_like(l_scr)
        acc_scr[...] = jnp.zeros_like(acc_scr)
        carry_scr[...] = jnp.zeros_like(carry_scr)

    tri_ones = jnp.concatenate(_tri_and_ones(), axis=1)
    qbd = qbd_ref[0]

    def pages(state, lfs, kts, vts, mask):
        m, l, acc, carry = state
        n = len(lfs)
        pad = (-H) % 16
        Hp = H + pad
        zeros = [jnp.zeros((pad, P), F32)] if pad else []
        terms = _split3(jnp.concatenate([piece for lf in lfs for piece in [lf] + zeros], axis=0))
        y = _dot(jnp.concatenate(terms, axis=0), tri_ones)
        y = y[0:n * Hp] + y[n * Hp:2 * n * Hp] + y[2 * n * Hp:3 * n * Hp]
        ss = []
        for i in range(n):
            fk = carry + y[i * Hp:i * Hp + H, :P]
            carry = carry + y[i * Hp:i * Hp + H, P:]
            s = _dot(qbd, kts[i]) - jnp.concatenate([fk * LOG2E] * T, axis=0)
            ss.append(s if mask is None else jnp.where(mask, s, NEG_BIG))
        m_new = jnp.maximum(m, jnp.max(functools.reduce(jnp.maximum, ss), axis=1, keepdims=True))
        alpha = jnp.exp2(m - m_new)
        ps = [jnp.exp2(s - m_new) for s in ss]
        l = alpha * l + jnp.sum(functools.reduce(jnp.add, ps), axis=1, keepdims=True)
        pv = [lax.dot_general(p.astype(BF16), vt, contract_last, preferred_element_type=F32)
              for p, vt in zip(ps, vts)]
        acc = alpha * acc + functools.reduce(jnp.add, pv)
        return m_new, l, acc, carry

    state = (m_scr[...], l_scr[...], acc_scr[...], carry_scr[...])
    state = pages(state, [r[0, 0] for r in lf_refs], [r[0, 0].astype(BF16) for r in k_refs],
                  [r[0, 0].astype(BF16) for r in v_refs], None)
    m_scr[...], l_scr[...], acc_scr[...], carry_scr[...] = state

    @pl.when(j == pl.num_programs(1) - 1)
    def _():
        t_row = lax.broadcasted_iota(jnp.int32, (R, P), 0) // H
        key = lax.broadcasted_iota(jnp.int32, (R, P), 1)
        _, l, acc, _ = pages(state, [lfnew_ref[0]], [knew_ref[0]], [vnew_ref[0]], key <= t_row)
        o_ref[0] = acc / l


def _attn_sample(page_table, qbd, k_new, v_new, lf_new, cache_k, cache_v, cache_lf, *, layer, T, H, pages_per_step=16):
    n_seq, n_pages = page_table.shape
    _, _, A, P = cache_k.shape
    R = T * H
    PP = _pick_tile(n_pages, pages_per_step, 1)

    def seq_spec(shape):
        return pl.BlockSpec((1,) + shape, lambda s, j, pt: (s, 0, 0))

    def page_spec(shape, p):
        return pl.BlockSpec((1, 1) + shape, lambda s, j, pt: (layer, pt[s, j * PP + p], 0, 0))

    in_specs = ([seq_spec((R, A)), seq_spec((A, P)), seq_spec((A, P)), seq_spec((H, P))]
                + [page_spec((A, P), p) for p in range(PP)] + [page_spec((A, P), p) for p in range(PP)]
                + [page_spec((H, P), p) for p in range(PP)])
    grid_spec = pltpu.PrefetchScalarGridSpec(
        num_scalar_prefetch=1, grid=(n_seq, n_pages // PP), in_specs=in_specs,
        out_specs=pl.BlockSpec((1, R, A), lambda s, j, pt: (s, 0, 0)),
        scratch_shapes=[pltpu.VMEM((R, 1), F32), pltpu.VMEM((R, 1), F32), pltpu.VMEM((R, A), F32),
                        pltpu.VMEM((H, P), F32)])
    return pl.pallas_call(
        functools.partial(_attn_sample_kernel, PP=PP, T=T, H=H, P=P), grid_spec=grid_spec,
        out_shape=jax.ShapeDtypeStruct((n_seq, R, A), F32),
        compiler_params=_params(("arbitrary", "arbitrary")), name="attn_sample")(
            page_table, qbd, k_new, v_new, lf_new, *([cache_k] * PP), *([cache_v] * PP), *([cache_lf] * PP))


def _merge_kernel(o_ref, cy_ref, sga_ref, sgc_ref, x_ref, wa_ref, wc_ref, wo_ref, out_ref):
    ya = _dot(o_ref[...], wa_ref[...])
    yc = _dot(cy_ref[...], wc_ref[...])
    mix = (sga_ref[...] * ya + sgc_ref[...] * yc).astype(BF16)
    out_ref[...] = x_ref[...] + _dot(mix, wo_ref[...])


def _merge(o, cy, sga, sgc, x, wa, wc, wo, *, tm_target=512):
    M, D = x.shape
    tm = _pick_tile(M, tm_target, 16)

    def row_spec(n):
        return pl.BlockSpec((tm, n), lambda i: (i, 0))

    def w_spec(a):
        return pl.BlockSpec(a.shape, lambda i: (0, 0))

    return pl.pallas_call(
        _merge_kernel, grid=(M // tm,),
        in_specs=[row_spec(o.shape[1]), row_spec(cy.shape[1]), row_spec(D), row_spec(D), row_spec(D),
                  w_spec(wa), w_spec(wc), w_spec(wo)],
        out_specs=row_spec(D), out_shape=jax.ShapeDtypeStruct((M, D), F32),
        compiler_params=_params(("arbitrary",)), name="merge")(o, cy, sga, sgc, x, wa, wc, wo)


def _swiglu_step(xb, wg, wu, wd, row_scale=None):
    gate = _dot(xb, wg)
    up = _dot(xb, wu)
    h = jax.nn.silu(gate) * up
    if row_scale is not None:
        h = h * row_scale
    return _dot(h.astype(BF16), wd)


def _finish(x, acc, gfin_ref, out_ref):
    y = x + acc
    if gfin_ref is not None:
        y = _rms(y, gfin_ref[...])
    out_ref[...] = y


def _ffn_kernel(*refs, final):
    if final:
        x_ref, g_ref, wg_ref, wu_ref, wd_ref, gfin_ref, out_ref, xn_scr, acc_scr = refs
    else:
        x_ref, g_ref, wg_ref, wu_ref, wd_ref, out_ref, xn_scr, acc_scr = refs
        gfin_ref = None
    f = pl.program_id(1)

    @pl.when(f == 0)
    def _():
        xn_scr[...] = _rms(x_ref[...], g_ref[...]).astype(BF16)
        acc_scr[...] = jnp.zeros_like(acc_scr)

    acc_scr[...] += _swiglu_step(xn_scr[...], wg_ref[...], wu_ref[...], wd_ref[...])

    @pl.when(f == pl.num_programs(1) - 1)
    def _():
        _finish(x_ref[...], acc_scr[...], gfin_ref, out_ref)


def _ffn(x, g, wg, wu, wd, gfin=None, *, tm_target=1024, tf_target=512):
    M, D = x.shape
    F = wg.shape[1]
    tm = _pick_tile(M, tm_target, 16)
    tf = _pick_tile(F, tf_target, LANES)
    final = gfin is not None
    in_specs = [pl.BlockSpec((tm, D), lambda i, f: (i, 0)), pl.BlockSpec((1, D), lambda i, f: (0, 0)),
                pl.BlockSpec((D, tf), lambda i, f: (0, f)), pl.BlockSpec((D, tf), lambda i, f: (0, f)),
                pl.BlockSpec((tf, D), lambda i, f: (f, 0))]
    args = [x, g, wg, wu, wd]
    if final:
        in_specs.append(pl.BlockSpec((1, D), lambda i, f: (0, 0)))
        args.append(gfin)
    return pl.pallas_call(
        functools.partial(_ffn_kernel, final=final), grid=(M // tm, F // tf), in_specs=in_specs,
        out_specs=pl.BlockSpec((tm, D), lambda i, f: (i, 0)), out_shape=jax.ShapeDtypeStruct((M, D), F32),
        scratch_shapes=[pltpu.VMEM((tm, D), BF16), pltpu.VMEM((tm, D), F32)],
        compiler_params=_params(("arbitrary", "arbitrary")), name="ffn_dense")(*args)


def _route(xn, wr_hi, wr_lo, n_experts):
    hi = xn.astype(BF16)
    lo = (xn - hi.astype(F32)).astype(BF16)
    logits = _dot(hi, wr_hi) + _dot(lo, wr_hi) + _dot(hi, wr_lo)
    lane = lax.broadcasted_iota(jnp.int32, logits.shape, 1).astype(F32)
    lg = jnp.where(lane < n_experts, logits, NEG_BIG)
    v1 = jnp.max(lg, axis=1, keepdims=True)
    i1 = jnp.min(jnp.where(lg == v1, lane, float(LANES)), axis=1, keepdims=True)
    lg2 = jnp.where(lane == i1, NEG_BIG, lg)
    v2 = jnp.max(lg2, axis=1, keepdims=True)
    i2 = jnp.min(jnp.where(lg2 == v2, lane, float(LANES)), axis=1, keepdims=True)
    e2 = jnp.exp(v2 - v1)
    den = 1.0 + e2
    return jnp.where(lane == i1, 1.0 / den, 0.0) + jnp.where(lane == i2, e2 / den, 0.0)


def _moe_dense_kernel(*refs, final, n_experts):
    if final:
        x_ref, g_ref, wrh_ref, wrl_ref, wg_ref, wu_ref, wd_ref, gfin_ref, out_ref, xn_scr, acc_scr, comb_scr = refs
    else:
        x_ref, g_ref, wrh_ref, wrl_ref, wg_ref, wu_ref, wd_ref, out_ref, xn_scr, acc_scr, comb_scr = refs
        gfin_ref = None
    e = pl.program_id(1)
    f = pl.program_id(2)

    @pl.when((e == 0) & (f == 0))
    def _():
        xn = _rms(x_ref[...], g_ref[...])
        xn_scr[...] = xn.astype(BF16)
        comb_scr[...] = _route(xn, wrh_ref[...], wrl_ref[...], n_experts)
        acc_scr[...] = jnp.zeros_like(acc_scr)

    comb = comb_scr[...]
    lane = lax.broadcasted_iota(jnp.int32, comb.shape, 1)
    ce = jnp.sum(jnp.where(lane == e, comb, 0.0), axis=1, keepdims=True)
    acc_scr[...] += _swiglu_step(xn_scr[...], wg_ref[0], wu_ref[0], wd_ref[0], row_scale=ce)

    @pl.when((e == pl.num_programs(1) - 1) & (f == pl.num_programs(2) - 1))
    def _():
        _finish(x_ref[...], acc_scr[...], gfin_ref, out_ref)


def _moe_dense(x, g, wr_hi, wr_lo, wg, wu, wd, gfin=None, *, tm_target=1024, tf_target=512):
    M, D = x.shape
    E, _, F = wg.shape
    tm = _pick_tile(M, tm_target, 16)
    tf = _pick_tile(F, tf_target, LANES)
    final = gfin is not None
    vec = pl.BlockSpec((1, D), lambda i, e, f: (0, 0))
    wr_spec = pl.BlockSpec((D, LANES), lambda i, e, f: (0, 0))
    in_specs = [pl.BlockSpec((tm, D), lambda i, e, f: (i, 0)), vec, wr_spec, wr_spec,
                pl.BlockSpec((1, D, tf), lambda i, e, f: (e, 0, f)), pl.BlockSpec((1, D, tf), lambda i, e, f: (e, 0, f)),
                pl.BlockSpec((1, tf, D), lambda i, e, f: (e, f, 0))]
    args = [x, g, wr_hi, wr_lo, wg, wu, wd]
    if final:
        in_specs.append(vec)
        args.append(gfin)
    return pl.pallas_call(
        functools.partial(_moe_dense_kernel, final=final, n_experts=E), grid=(M // tm, E, F // tf),
        in_specs=in_specs, out_specs=pl.BlockSpec((tm, D), lambda i, e, f: (i, 0)),
        out_shape=jax.ShapeDtypeStruct((M, D), F32),
        scratch_shapes=[pltpu.VMEM((tm, D), BF16), pltpu.VMEM((tm, D), F32), pltpu.VMEM((tm, LANES), F32)],
        compiler_params=_params(("arbitrary", "arbitrary", "arbitrary")), name="moe_dense")(*args)


def kernel(x_prompt, x_sample, cache_k, cache_v, cache_logf, state_conv, page_table, meta_tokens, norm_mix, w_in,
           b_forget, conv_w, w_out_attn, w_out_conv, w_o, norm_ffn, ffn_w_gate, ffn_w_up, ffn_w_down, moe_router,
           moe_w_gate, moe_w_up, moe_w_down, norm_final):
    B, S, D = x_prompt.shape
    NB, T, _ = x_sample.shape
    depth, n_pool, P, H, Dh = cache_k.shape
    n_meta = meta_tokens.shape[0]
    A = H * Dh
    C = conv_w.shape[2]
    halo = conv_w.shape[1] - 1
    E = moe_router.shape[2]
    L = S + n_meta
    Lp = -(-L // 256) * 256
    n_chunks = Lp // LANES
    scale = float(Dh) ** -0.5 * LOG2E

    o_f = 3 * A
    o_c = o_f + H

    def prep_w_in(w):
        wf = jnp.pad(w[:, o_f:o_c], ((0, 0), (0, LANES - H)))
        return jnp.concatenate([w[:, :o_f], w[:, o_c:], wf], axis=1).astype(BF16)

    def row(v):
        return v.reshape(1, -1)

    meta = jnp.broadcast_to(meta_tokens[None].astype(x_prompt.dtype), (B, n_meta, D))
    x_p = jnp.concatenate([meta, x_prompt, jnp.zeros((B, Lp - L, D), x_prompt.dtype)], axis=1).reshape(B * Lp, D)
    x_s = x_sample.reshape(NB * T, D)
    eye_h = jnp.eye(H, dtype=F32)

    outs = {n: [] for n in ("lfp", "cp", "ks", "vs", "lfs", "cs")}
    kv_prev = ()
    cache_kt = cache_k.transpose(0, 1, 3, 4, 2).reshape(depth, n_pool, A, P)
    cache_vt = cache_v.transpose(0, 1, 3, 4, 2).reshape(depth, n_pool, A, P)
    cache_lft = cache_logf.transpose(0, 1, 3, 2)
    for l in range(depth):
        last = l == depth - 1
        w_l = prep_w_in(w_in[l])
        bf_l = jnp.pad(b_forget[l], (0, LANES - H)).reshape(1, LANES)
        g_l = row(norm_mix[l])
        wa, wc, wo = w_out_attn[l].astype(BF16), w_out_conv[l].astype(BF16), w_o[l].astype(BF16)

        q, k_all, v_all, kb, vb, lf, cy, sga, sgc, cst = _inproj(
            x_p, g_l, w_l, bf_l, conv_w[l], A=A, C=C, H=H, scale=scale, seq_len=Lp, valid_len=L, layer=l,
            depth=depth, kv_prev=kv_prev)
        kv_prev = (k_all, v_all)
        lf_chunks = lf.reshape(B, n_chunks, LANES, H).transpose(1, 0, 3, 2).reshape(n_chunks, B * H, LANES)
        fk = _cumsum_chunks(lf_chunks).reshape(n_chunks, B, H, LANES).transpose(1, 2, 0, 3).reshape(B, H, Lp)
        o_a = _attn_prompt(q.reshape(B, Lp, A), kb.reshape(B, Lp, A), vb.reshape(B, Lp, A), fk, H=H, Dh=Dh)
        x_p = _merge(o_a.reshape(B * Lp, A), cy, sga, sgc, x_p, wa, wc, wo)
        outs["lfp"].append(lf.reshape(B, Lp, H)[:, :L])
        outs["cp"].append(cst)

        st = state_conv[l]
        halos = [jnp.pad(st[:, halo - s:], ((0, 0), (0, T - s), (0, 0))).reshape(NB * T, C) for s in range(1, halo + 1)]
        q, k, v, kb, vb, lf, cy, sga, sgc, u = _inproj(
            x_s, g_l, w_l, bf_l, conv_w[l], A=A, C=C, H=H, scale=scale, T=T, halos=halos)
        qbd = (q.reshape(NB, T, H, 1, Dh) * eye_h.astype(BF16)[None, None, :, :, None]).reshape(NB, T * H, A)
        pad_keys = ((0, 0), (0, 0), (0, P - T))
        o_bd = _attn_sample(
            page_table, qbd, jnp.pad(kb.reshape(NB, T, A).transpose(0, 2, 1), pad_keys),
            jnp.pad(vb.reshape(NB, T, A).transpose(0, 2, 1), pad_keys),
            jnp.pad(lf.reshape(NB, T, H).transpose(0, 2, 1), pad_keys),
            cache_kt, cache_vt, cache_lft, layer=l, T=T, H=H)
        o_a = jnp.sum(o_bd.reshape(NB, T, H, H, Dh) * eye_h[None, None, :, :, None], axis=2).reshape(NB * T, A)
        x_s = _merge(o_a.astype(BF16), cy, sga, sgc, x_s, wa, wc, wo)
        outs["ks"].append(k.reshape(NB, T, H, Dh))
        outs["vs"].append(v.reshape(NB, T, H, Dh))
        outs["lfs"].append(lf.reshape(NB, T, H))
        outs["cs"].append(u.reshape(NB, T, C)[:, T - halo:])

        gf = row(norm_ffn[l])
        gfin = row(norm_final) if last else None
        j = l // 2
        if l % 2 == 0:
            wg, wu, wd = ffn_w_gate[j].astype(BF16), ffn_w_up[j].astype(BF16), ffn_w_down[j].astype(BF16)
            x_p = _ffn(x_p, gf, wg, wu, wd, gfin)
            x_s = _ffn(x_s, gf, wg, wu, wd, gfin)
        else:
            wr = jnp.pad(moe_router[j], ((0, 0), (0, LANES - E)))
            wr_hi = wr.astype(BF16)
            wr_lo = (wr - wr_hi.astype(F32)).astype(BF16)
            wg, wu, wd = moe_w_gate[j].astype(BF16), moe_w_up[j].astype(BF16), moe_w_down[j].astype(BF16)
            x_p = _moe_dense(x_p, gf, wr_hi, wr_lo, wg, wu, wd, gfin)
            x_s = _moe_dense(x_s, gf, wr_hi, wr_lo, wg, wu, wd, gfin)

    y_prompt = x_p.reshape(B, Lp, D)[:, n_meta:L]
    y_sample = x_s.reshape(NB, T, D)
    return (y_prompt, y_sample, kv_prev[0][:, :, :L], kv_prev[1][:, :, :L], jnp.stack(outs["lfp"]),
            jnp.stack(outs["cp"]), jnp.stack(outs["ks"]), jnp.stack(outs["vs"]), jnp.stack(outs["lfs"]),
            jnp.stack(outs["cs"]))
```

```python
import functools

import jax
import jax.numpy as jnp
from jax import lax
from jax.experimental import pallas as pl
from jax.experimental.pallas import tpu as pltpu

RMS_EPS = 1e-6
TOP_K = 2
NEG_BIG = -1e30
LOG2E = 1.4426950408889634
LANES = 128
V7X_VMEM_LIMIT_BYTES = 56 * 1024 * 1024

F32 = jnp.float32
BF16 = jnp.bfloat16


def _pick_tile(n, target, mult=8):
    best = None
    for t in range(mult, min(n, target) + 1, mult):
        if n % t == 0:
            best = t
    assert best is not None, (n, target, mult)
    return best


def _params(sem, vmem=V7X_VMEM_LIMIT_BYTES):
    return pltpu.CompilerParams(dimension_semantics=sem, vmem_limit_bytes=vmem)


def _rms(x, g):
    ms = jnp.mean(x * x, axis=-1, keepdims=True)
    return (x * lax.rsqrt(ms + RMS_EPS)) * g


def _log_sigmoid(x):
    return jnp.minimum(x, 0.0) - jnp.log1p(jnp.exp(-jnp.abs(x)))


def _split3(x):
    hi = x.astype(BF16)
    r1 = x - hi.astype(F32)
    mid = r1.astype(BF16)
    lo = (r1 - mid.astype(F32)).astype(BF16)
    return hi, mid, lo


def _dot(a, b):
    return jnp.dot(a, b, preferred_element_type=F32)


def _dot3(x, w_bf16):
    hi, mid, lo = _split3(x)
    return _dot(hi, w_bf16) + _dot(mid, w_bf16) + _dot(lo, w_bf16)


def _inproj_kernel(*refs, tm, A, C, D, H, scale, halo, sample, tiles_per_seq, T, cst_tile, cst_row):
    if sample:
        (x_ref, g_ref, w_ref, bf_ref, cw_ref, *halo_refs) = refs[: 5 + halo]
        (q_ref, k_ref, v_ref, kb_ref, vb_ref, lf_ref, cy_ref, sga_ref, sgc_ref, u_ref) = refs[5 + halo:]
    else:
        (x_ref, g_ref, w_ref, bf_ref, cw_ref,
         q_ref, k_ref, v_ref, kb_ref, vb_ref, lf_ref, cy_ref, sga_ref, sgc_ref, cst_ref, carry_scr) = refs

    xb = _rms(x_ref[...], g_ref[...]).astype(BF16)

    def mm(lo, hi):
        return _dot(xb, w_ref[:, lo:hi])

    zq = mm(0, 3 * A)
    q_ref[...] = (zq[:, :A] * scale).astype(BF16)
    k = zq[:, A:2 * A]
    v = zq[:, 2 * A:3 * A]
    if sample:
        k_ref[...] = k
        v_ref[...] = v
    else:
        k_ref[0] = k.reshape(tm, H, A // H)
        v_ref[0] = v.reshape(tm, H, A // H)
    kb_ref[...] = k.astype(BF16)
    vb_ref[...] = v.astype(BF16)

    o = 3 * A
    zc = mm(o, o + 3 * C)
    xin, bg, cg = zc[:, :C], zc[:, C:2 * C], zc[:, 2 * C:]
    u = cg * xin
    row = lax.broadcasted_iota(jnp.int32, (tm, C), 0)
    shifted = [u]
    if sample:
        pos = row % T
        for s in range(1, halo + 1):
            shifted.append(jnp.where(pos >= s, pltpu.roll(u, s, 0), halo_refs[s - 1][...]))
        u_ref[...] = u
    else:
        t = pl.program_id(0) % tiles_per_seq

        @pl.when(t == 0)
        def _():
            carry_scr[...] = jnp.zeros_like(carry_scr)

        for s in range(1, halo + 1):
            us = pltpu.roll(u, s, 0)
            for r in range(s):
                us = jnp.where(row == r, carry_scr[halo - s + r:halo - s + r + 1, :], us)
            shifted.append(us)
    y = cw_ref[0:1, :] * shifted[halo]
    for i in range(1, halo + 1):
        y = y + cw_ref[i:i + 1, :] * shifted[halo - i]
    cy_ref[...] = (bg * y).astype(BF16)
    if not sample:
        carry_scr[0:halo, :] = u[tm - halo:tm, :]

        @pl.when(t == cst_tile)
        def _():
            cst_ref[0] = u[cst_row:cst_row + halo, :]

    o += 3 * C
    zg = mm(o, o + 2 * D)
    sga_ref[...] = jax.nn.sigmoid(zg[:, :D])
    sgc_ref[...] = jax.nn.sigmoid(zg[:, D:])

    o += 2 * D
    zf = mm(o, o + LANES)
    lf_ref[...] = _log_sigmoid(zf + bf_ref[...])[:, :H]


def _inproj(x, g, w, bf, cw, *, A, C, H, scale, seq_len=None, valid_len=None, T=None, halos=None, tm_target=384):
    M, D = x.shape
    halo = cw.shape[0] - 1
    sample = halos is not None
    if sample:
        tm = M
        tiles_per_seq, cst_tile, cst_row = 1, 0, 0
        assert T >= halo
    else:
        tm = _pick_tile(seq_len, tm_target, 16)
        tiles_per_seq = seq_len // tm
        cst_tile, cst_row = divmod(valid_len - halo, tm)
        assert cst_row + halo <= tm
    n_seq = M // (tiles_per_seq * tm)
    Nw = w.shape[1]

    def row_spec(n):
        return pl.BlockSpec((tm, n), lambda i: (i, 0))

    def full_spec(a):
        return pl.BlockSpec(a.shape, lambda i: (0,) * a.ndim)

    in_specs = [row_spec(D), full_spec(g), pl.BlockSpec((D, Nw), lambda i: (0, 0), pipeline_mode=pl.Buffered(1)),
                full_spec(bf), full_spec(cw)]
    args = [x, g, w, bf, cw]
    if sample:
        in_specs += [row_spec(C) for _ in halos]
        args += list(halos)
    if sample:
        kv_shape = jax.ShapeDtypeStruct((M, A), F32)
        kv_spec = row_spec(A)
    else:
        kv_shape = jax.ShapeDtypeStruct((n_seq, seq_len, H, A // H), F32)
        kv_spec = pl.BlockSpec((1, tm, H, A // H), lambda i: (i // tiles_per_seq, i % tiles_per_seq, 0, 0))
    out_shape = [jax.ShapeDtypeStruct((M, A), BF16), kv_shape, kv_shape,
                 jax.ShapeDtypeStruct((M, A), BF16), jax.ShapeDtypeStruct((M, A), BF16), jax.ShapeDtypeStruct((M, H), F32),
                 jax.ShapeDtypeStruct((M, C), BF16), jax.ShapeDtypeStruct((M, D), F32), jax.ShapeDtypeStruct((M, D), F32)]
    out_specs = [row_spec(A), kv_spec, kv_spec, row_spec(A), row_spec(A), row_spec(H), row_spec(C), row_spec(D),
                 row_spec(D)]
    scratch = []
    if sample:
        out_shape.append(jax.ShapeDtypeStruct((M, C), F32))
        out_specs.append(row_spec(C))
    else:
        out_shape.append(jax.ShapeDtypeStruct((n_seq, halo, C), F32))
        out_specs.append(pl.BlockSpec((1, halo, C), lambda i: (i // tiles_per_seq, 0, 0)))
        scratch.append(pltpu.VMEM((8, C), F32))
    kern = functools.partial(_inproj_kernel, tm=tm, A=A, C=C, D=D, H=H, scale=scale, halo=halo, sample=sample,
                             tiles_per_seq=tiles_per_seq, T=T, cst_tile=cst_tile, cst_row=cst_row)
    return pl.pallas_call(
        kern, grid=(M // tm,), in_specs=in_specs, out_specs=out_specs, out_shape=out_shape,
        scratch_shapes=scratch, compiler_params=_params(("arbitrary",)),
        name="inproj_sample" if sample else "inproj_prompt")(*args)


def _tri_and_ones():
    i = lax.broadcasted_iota(jnp.int32, (LANES, LANES), 0)
    j = lax.broadcasted_iota(jnp.int32, (LANES, LANES), 1)
    return jnp.where(i <= j, 1.0, 0.0).astype(BF16), jnp.ones((LANES, LANES), BF16)


def _cumsum_kernel(x_ref, o_ref, *, n_chunks, R):
    tri, ones = _tri_and_ones()
    x = x_ref[...].reshape(n_chunks * R, LANES)
    hi, mid, lo = _split3(x)
    y = _dot(hi, tri) + _dot(mid, tri) + _dot(lo, tri)
    tot = _dot(hi, ones) + _dot(mid, ones) + _dot(lo, ones)
    carry = jnp.zeros((R, LANES), F32)
    for c in range(n_chunks):
        o_ref[c] = (y[c * R:(c + 1) * R] + carry) * LOG2E
        carry = carry + tot[c * R:(c + 1) * R]


def _cumsum_chunks(x):
    n_chunks, R, _ = x.shape
    return pl.pallas_call(
        functools.partial(_cumsum_kernel, n_chunks=n_chunks, R=R),
        out_shape=jax.ShapeDtypeStruct(x.shape, F32), name="logf_cumsum")(x)


def _attn_kernel(q_ref, k_ref, v_ref, fk_ref, o_ref, qm_scr, mrun_scr, m_scr, lrun_scr, acc_scr, *, tq, tk, H, Dh):
    qi = pl.program_id(1)
    r = tq // tk
    n_full = qi * r
    n_sub = tk // LANES
    G = LANES // Dh
    contract_last = (((1,), (1,)), ((), ()))
    lane = lax.broadcasted_iota(jnp.int32, (tq, LANES), 1)

    def group(h):
        return slice((h // G) * LANES, (h // G + 1) * LANES)

    def own_lanes(h):
        return (lane >= (h % G) * Dh) & (lane < (h % G + 1) * Dh)

    for h in range(H):
        qg = q_ref[0, :, group(h)]
        qm_scr[h] = jnp.where(own_lanes(h), qg, jnp.zeros_like(qg))

    def scores(h, j, masked):
        start = pl.multiple_of(j * tk, tk)
        s = lax.dot_general(qm_scr[h], k_ref[0, pl.ds(start, tk), group(h)], contract_last,
                            preferred_element_type=F32)
        s = s - fk_ref[0, h:h + 1, pl.ds(start, tk)]
        if masked:
            qpos = qi * tq + lax.broadcasted_iota(jnp.int32, (tq, tk), 0)
            kpos = start + lax.broadcasted_iota(jnp.int32, (tq, tk), 1)
            s = jnp.where(kpos <= qpos, s, NEG_BIG)
        return [s[:, c * LANES:(c + 1) * LANES] for c in range(n_sub)], start

    def max_pass(j, masked):
        for h in range(H):
            parts, _ = scores(h, j, masked)
            mrun_scr[h] = functools.reduce(jnp.maximum, parts, mrun_scr[h])

    def sum_pass(j, masked):
        for h in range(H):
            parts, start = scores(h, j, masked)
            m = m_scr[h]
            probs = [jnp.exp2(part - m) for part in parts]
            lrun_scr[h] = functools.reduce(jnp.add, probs, lrun_scr[h])
            p = jnp.concatenate(probs, axis=1).astype(BF16)
            acc_scr[h] += _dot(p, v_ref[0, pl.ds(start, tk), group(h)])

    def run(step):
        def body(jj, carry):
            step(2 * jj, False)
            step(2 * jj + 1, False)
            return carry

        lax.fori_loop(0, n_full // 2, body, 0)

        @pl.when(n_full % 2 == 1)
        def _():
            step(n_full - 1, False)

        for d in range(r):
            step(n_full + d, True)

    mrun_scr[...] = jnp.full_like(mrun_scr, NEG_BIG)
    run(max_pass)
    for h in range(H):
        m_scr[h] = jnp.broadcast_to(jnp.max(mrun_scr[h], axis=1, keepdims=True), (tq, LANES))
    lrun_scr[...] = jnp.zeros_like(lrun_scr)
    acc_scr[...] = jnp.zeros_like(acc_scr)
    run(sum_pass)
    for g in range(H // G):
        out = jnp.zeros((tq, LANES), F32)
        for h in range(g * G, (g + 1) * G):
            l = jnp.sum(lrun_scr[h], axis=1, keepdims=True)
            out = jnp.where(own_lanes(h), acc_scr[h] / l, out)
        o_ref[0, :, group(g * G)] = out.astype(BF16)


def _attn_prompt(q, k, v, fk, *, H, Dh, tq=256, tk=256):
    B, Lp, A = q.shape
    assert LANES % Dh == 0 and H % (LANES // Dh) == 0
    kv_spec = pl.BlockSpec((1, Lp, A), lambda b, i: (b, 0, 0), pipeline_mode=pl.Buffered(1))
    stat = pltpu.VMEM((H, tq, LANES), F32)
    return pl.pallas_call(
        functools.partial(_attn_kernel, tq=tq, tk=tk, H=H, Dh=Dh),
        grid=(B, Lp // tq),
        in_specs=[pl.BlockSpec((1, tq, A), lambda b, i: (b, i, 0)), kv_spec, kv_spec,
                  pl.BlockSpec((1, H, Lp), lambda b, i: (b, 0, 0))],
        out_specs=pl.BlockSpec((1, tq, A), lambda b, i: (b, i, 0)),
        out_shape=jax.ShapeDtypeStruct((B, Lp, A), BF16),
        scratch_shapes=[pltpu.VMEM((H, tq, LANES), BF16), stat, stat, stat, stat],
        compiler_params=_params(("arbitrary", "arbitrary")), name="attn_prompt")(q, k, v, fk)


def _attn_sample_kernel(pt_ref, qbd_ref, knew_ref, vnew_ref, lfnew_ref, *refs, PP, T, H, P):
    k_refs, v_refs, lf_refs = refs[:PP], refs[PP:2 * PP], refs[2 * PP:3 * PP]
    o_ref, m_scr, l_scr, acc_scr, carry_scr = refs[3 * PP:]
    j = pl.program_id(1)
    R = T * H
    contract_last = (((1,), (1,)), ((), ()))

    @pl.when(j == 0)
    def _():
        m_scr[...] = jnp.full_like(m_scr, NEG_BIG)
        l_scr[...] = jnp.zeros_like(l_scr)
        acc_scr[...] = jnp.zeros_like(acc_scr)
        carry_scr[...] = jnp.zeros_like(carry_scr)

    tri_ones = jnp.concatenate(_tri_and_ones(), axis=1)
    qbd = qbd_ref[0]

    def pages(state, lfs, kts, vts, mask):
        m, l, acc, carry = state
        n = len(lfs)
        pad = (-H) % 16
        Hp = H + pad
        zeros = [jnp.zeros((pad, P), F32)] if pad else []
        terms = _split3(jnp.concatenate([piece for lf in lfs for piece in [lf] + zeros], axis=0))
        y = _dot(jnp.concatenate(terms, axis=0), tri_ones)
        y = y[0:n * Hp] + y[n * Hp:2 * n * Hp] + y[2 * n * Hp:3 * n * Hp]
        ss = []
        for i in range(n):
            fk = carry + y[i * Hp:i * Hp + H, :P]
            carry = carry + y[i * Hp:i * Hp + H, P:]
            s = _dot(qbd, kts[i]) - jnp.concatenate([fk * LOG2E] * T, axis=0)
            ss.append(s if mask is None else jnp.where(mask, s, NEG_BIG))
        m_new = jnp.maximum(m, jnp.max(functools.reduce(jnp.maximum, ss), axis=1, keepdims=True))
        alpha = jnp.exp2(m - m_new)
        ps = [jnp.exp2(s - m_new) for s in ss]
        l = alpha * l + jnp.sum(functools.reduce(jnp.add, ps), axis=1, keepdims=True)
        pv = [lax.dot_general(p.astype(BF16), vt, contract_last, preferred_element_type=F32)
              for p, vt in zip(ps, vts)]
        acc = alpha * acc + functools.reduce(jnp.add, pv)
        return m_new, l, acc, carry

    state = (m_scr[...], l_scr[...], acc_scr[...], carry_scr[...])
    state = pages(state, [r[0, 0] for r in lf_refs], [r[0, 0].astype(BF16) for r in k_refs],
                  [r[0, 0].astype(BF16) for r in v_refs], None)
    m_scr[...], l_scr[...], acc_scr[...], carry_scr[...] = state

    @pl.when(j == pl.num_programs(1) - 1)
    def _():
        t_row = lax.broadcasted_iota(jnp.int32, (R, P), 0) // H
        key = lax.broadcasted_iota(jnp.int32, (R, P), 1)
        _, l, acc, _ = pages(state, [lfnew_ref[0]], [knew_ref[0]], [vnew_ref[0]], key <= t_row)
        o_ref[0] = acc / l


def _attn_sample(page_table, qbd, k_new, v_new, lf_new, cache_k, cache_v, cache_lf, *, layer, T, H, pages_per_step=16):
    n_seq, n_pages = page_table.shape
    _, _, A, P = cache_k.shape
    R = T * H
    PP = _pick_tile(n_pages, pages_per_step, 1)

    def seq_spec(shape):
        return pl.BlockSpec((1,) + shape, lambda s, j, pt: (s, 0, 0))

    def page_spec(shape, p):
        return pl.BlockSpec((1, 1) + shape, lambda s, j, pt: (layer, pt[s, j * PP + p], 0, 0))

    in_specs = ([seq_spec((R, A)), seq_spec((A, P)), seq_spec((A, P)), seq_spec((H, P))]
                + [page_spec((A, P), p) for p in range(PP)] + [page_spec((A, P), p) for p in range(PP)]
                + [page_spec((H, P), p) for p in range(PP)])
    grid_spec = pltpu.PrefetchScalarGridSpec(
        num_scalar_prefetch=1, grid=(n_seq, n_pages // PP), in_specs=in_specs,
        out_specs=pl.BlockSpec((1, R, A), lambda s, j, pt: (s, 0, 0)),
        scratch_shapes=[pltpu.VMEM((R, 1), F32), pltpu.VMEM((R, 1), F32), pltpu.VMEM((R, A), F32),
                        pltpu.VMEM((H, P), F32)])
    return pl.pallas_call(
        functools.partial(_attn_sample_kernel, PP=PP, T=T, H=H, P=P), grid_spec=grid_spec,
        out_shape=jax.ShapeDtypeStruct((n_seq, R, A), F32),
        compiler_params=_params(("arbitrary", "arbitrary")), name="attn_sample")(
            page_table, qbd, k_new, v_new, lf_new, *([cache_k] * PP), *([cache_v] * PP), *([cache_lf] * PP))


def _merge_kernel(o_ref, cy_ref, sga_ref, sgc_ref, x_ref, wa_ref, wc_ref, wo_ref, out_ref):
    ya = _dot(o_ref[...], wa_ref[...])
    yc = _dot(cy_ref[...], wc_ref[...])
    mix = (sga_ref[...] * ya + sgc_ref[...] * yc).astype(BF16)
    out_ref[...] = x_ref[...] + _dot(mix, wo_ref[...])


def _merge(o, cy, sga, sgc, x, wa, wc, wo, *, tm_target=512):
    M, D = x.shape
    tm = _pick_tile(M, tm_target, 16)

    def row_spec(n):
        return pl.BlockSpec((tm, n), lambda i: (i, 0))

    def w_spec(a):
        return pl.BlockSpec(a.shape, lambda i: (0, 0))

    return pl.pallas_call(
        _merge_kernel, grid=(M // tm,),
        in_specs=[row_spec(o.shape[1]), row_spec(cy.shape[1]), row_spec(D), row_spec(D), row_spec(D),
                  w_spec(wa), w_spec(wc), w_spec(wo)],
        out_specs=row_spec(D), out_shape=jax.ShapeDtypeStruct((M, D), F32),
        compiler_params=_params(("arbitrary",)), name="merge")(o, cy, sga, sgc, x, wa, wc, wo)


def _swiglu_step(xb, wg, wu, wd, row_scale=None):
    gate = _dot(xb, wg)
    up = _dot(xb, wu)
    h = jax.nn.silu(gate) * up
    if row_scale is not None:
        h = h * row_scale
    return _dot(h.astype(BF16), wd)


def _finish(x, acc, gfin_ref, out_ref):
    y = x + acc
    if gfin_ref is not None:
        y = _rms(y, gfin_ref[...])
    out_ref[...] = y


def _ffn_kernel(*refs, final):
    if final:
        x_ref, g_ref, wg_ref, wu_ref, wd_ref, gfin_ref, out_ref, xn_scr, acc_scr = refs
    else:
        x_ref, g_ref, wg_ref, wu_ref, wd_ref, out_ref, xn_scr, acc_scr = refs
        gfin_ref = None
    f = pl.program_id(1)

    @pl.when(f == 0)
    def _():
        xn_scr[...] = _rms(x_ref[...], g_ref[...]).astype(BF16)
        acc_scr[...] = jnp.zeros_like(acc_scr)

    acc_scr[...] += _swiglu_step(xn_scr[...], wg_ref[...], wu_ref[...], wd_ref[...])

    @pl.when(f == pl.num_programs(1) - 1)
    def _():
        _finish(x_ref[...], acc_scr[...], gfin_ref, out_ref)


def _ffn(x, g, wg, wu, wd, gfin=None, *, tm_target=1024, tf_target=512):
    M, D = x.shape
    F = wg.shape[1]
    tm = _pick_tile(M, tm_target, 16)
    tf = _pick_tile(F, tf_target, LANES)
    final = gfin is not None
    in_specs = [pl.BlockSpec((tm, D), lambda i, f: (i, 0)), pl.BlockSpec((1, D), lambda i, f: (0, 0)),
                pl.BlockSpec((D, tf), lambda i, f: (0, f)), pl.BlockSpec((D, tf), lambda i, f: (0, f)),
                pl.BlockSpec((tf, D), lambda i, f: (f, 0))]
    args = [x, g, wg, wu, wd]
    if final:
        in_specs.append(pl.BlockSpec((1, D), lambda i, f: (0, 0)))
        args.append(gfin)
    return pl.pallas_call(
        functools.partial(_ffn_kernel, final=final), grid=(M // tm, F // tf), in_specs=in_specs,
        out_specs=pl.BlockSpec((tm, D), lambda i, f: (i, 0)), out_shape=jax.ShapeDtypeStruct((M, D), F32),
        scratch_shapes=[pltpu.VMEM((tm, D), BF16), pltpu.VMEM((tm, D), F32)],
        compiler_params=_params(("arbitrary", "arbitrary")), name="ffn_dense")(*args)


def _route(xn, wr_hi, wr_lo, n_experts):
    hi = xn.astype(BF16)
    lo = (xn - hi.astype(F32)).astype(BF16)
    logits = _dot(hi, wr_hi) + _dot(lo, wr_hi) + _dot(hi, wr_lo)
    lane = lax.broadcasted_iota(jnp.int32, logits.shape, 1).astype(F32)
    lg = jnp.where(lane < n_experts, logits, NEG_BIG)
    v1 = jnp.max(lg, axis=1, keepdims=True)
    i1 = jnp.min(jnp.where(lg == v1, lane, float(LANES)), axis=1, keepdims=True)
    lg2 = jnp.where(lane == i1, NEG_BIG, lg)
    v2 = jnp.max(lg2, axis=1, keepdims=True)
    i2 = jnp.min(jnp.where(lg2 == v2, lane, float(LANES)), axis=1, keepdims=True)
    e2 = jnp.exp(v2 - v1)
    den = 1.0 + e2
    return jnp.where(lane == i1, 1.0 / den, 0.0) + jnp.where(lane == i2, e2 / den, 0.0)


def _moe_dense_kernel(*refs, final, n_experts):
    if final:
        x_ref, g_ref, wrh_ref, wrl_ref, wg_ref, wu_ref, wd_ref, gfin_ref, out_ref, xn_scr, acc_scr, comb_scr = refs
    else:
        x_ref, g_ref, wrh_ref, wrl_ref, wg_ref, wu_ref, wd_ref, out_ref, xn_scr, acc_scr, comb_scr = refs
        gfin_ref = None
    e = pl.program_id(1)
    f = pl.program_id(2)

    @pl.when((e == 0) & (f == 0))
    def _():
        xn = _rms(x_ref[...], g_ref[...])
        xn_scr[...] = xn.astype(BF16)
        comb_scr[...] = _route(xn, wrh_ref[...], wrl_ref[...], n_experts)
        acc_scr[...] = jnp.zeros_like(acc_scr)

    comb = comb_scr[...]
    lane = lax.broadcasted_iota(jnp.int32, comb.shape, 1)
    ce = jnp.sum(jnp.where(lane == e, comb, 0.0), axis=1, keepdims=True)
    acc_scr[...] += _swiglu_step(xn_scr[...], wg_ref[0], wu_ref[0], wd_ref[0], row_scale=ce)

    @pl.when((e == pl.num_programs(1) - 1) & (f == pl.num_programs(2) - 1))
    def _():
        _finish(x_ref[...], acc_scr[...], gfin_ref, out_ref)


def _moe_dense(x, g, wr_hi, wr_lo, wg, wu, wd, gfin=None, *, tm_target=1024, tf_target=512):
    M, D = x.shape
    E, _, F = wg.shape
    tm = _pick_tile(M, tm_target, 16)
    tf = _pick_tile(F, tf_target, LANES)
    final = gfin is not None
    vec = pl.BlockSpec((1, D), lambda i, e, f: (0, 0))
    wr_spec = pl.BlockSpec((D, LANES), lambda i, e, f: (0, 0))
    in_specs = [pl.BlockSpec((tm, D), lambda i, e, f: (i, 0)), vec, wr_spec, wr_spec,
                pl.BlockSpec((1, D, tf), lambda i, e, f: (e, 0, f)), pl.BlockSpec((1, D, tf), lambda i, e, f: (e, 0, f)),
                pl.BlockSpec((1, tf, D), lambda i, e, f: (e, f, 0))]
    args = [x, g, wr_hi, wr_lo, wg, wu, wd]
    if final:
        in_specs.append(vec)
        args.append(gfin)
    return pl.pallas_call(
        functools.partial(_moe_dense_kernel, final=final, n_experts=E), grid=(M // tm, E, F // tf),
        in_specs=in_specs, out_specs=pl.BlockSpec((tm, D), lambda i, e, f: (i, 0)),
        out_shape=jax.ShapeDtypeStruct((M, D), F32),
        scratch_shapes=[pltpu.VMEM((tm, D), BF16), pltpu.VMEM((tm, D), F32), pltpu.VMEM((tm, LANES), F32)],
        compiler_params=_params(("arbitrary", "arbitrary", "arbitrary")), name="moe_dense")(*args)


SEG_ROWS = 16


def _selected(comb):
    return jnp.where(comb > 0.0, 1.0, 0.0)


def _positions(sel_bf16, tm):
    r = lax.broadcasted_iota(jnp.int32, (tm, tm), 0)
    c = lax.broadcasted_iota(jnp.int32, (tm, tm), 1)
    return _dot(jnp.where(c < r, 1.0, 0.0).astype(BF16), sel_bf16)


def _moe_route_kernel(x_ref, g_ref, wrh_ref, wrl_ref, xn_ref, comb_ref, cnt_ref, *, n_experts):
    xn = _rms(x_ref[...], g_ref[...])
    xn_ref[...] = xn.astype(BF16)
    comb = _route(xn, wrh_ref[...], wrl_ref[...], n_experts)
    comb_ref[...] = comb
    cnt_ref[0] = jnp.sum(_selected(comb), axis=0, keepdims=True)


def _segment_copies(meta, tile, n_experts, make_copy, wait):
    cnt_ref, loff_ref, goff_ref = meta
    for e in range(n_experts):
        idx = tile * n_experts + e
        local0, global0 = loff_ref[idx], goff_ref[idx]

        def body(k, carry, local0=local0, global0=global0):
            cp = make_copy(pl.multiple_of(local0 + k * SEG_ROWS, SEG_ROWS),
                           pl.multiple_of(global0 + k * SEG_ROWS, SEG_ROWS))
            if wait:
                cp.wait()
            else:
                cp.start()
            return carry

        lax.fori_loop(0, cnt_ref[idx] // SEG_ROWS, body, 0)


def _moe_dispatch_kernel(cnt_ref, loff_ref, goff_ref, xn_ref, comb_ref, loffv_ref, xs_in_ref, xs_ref,
                         stage_scr, sem, *, tm, S, n_experts):
    del xs_in_ref
    tile = pl.program_id(0)
    sel = _selected(comb_ref[...])
    slot = loffv_ref[0] + _positions(sel.astype(BF16), tm)
    slot_t = jnp.where(sel > 0.0, slot, -1.0).T
    rows = lax.broadcasted_iota(jnp.int32, (S, tm), 0).astype(F32)
    onehot = jnp.zeros((S, tm), F32)
    for e in range(n_experts):
        onehot = onehot + jnp.where(slot_t[e:e + 1, :] == rows, 1.0, 0.0)
    stage_scr[...] = _dot(onehot.astype(BF16), xn_ref[...]).astype(BF16)

    def make_copy(local, glob):
        return pltpu.make_async_copy(stage_scr.at[pl.ds(local, SEG_ROWS)], xs_ref.at[pl.ds(glob, SEG_ROWS)], sem)

    meta = (cnt_ref, loff_ref, goff_ref)
    _segment_copies(meta, tile, n_experts, make_copy, wait=False)
    _segment_copies(meta, tile, n_experts, make_copy, wait=True)


def _moe_group_kernel(blk_ref, nact_ref, xs_ref, wg_ref, wu_ref, wd_ref, ys_ref, acc_scr):
    del blk_ref
    i = pl.program_id(0)
    f = pl.program_id(1)
    active = i < nact_ref[0]

    @pl.when(f == 0)
    def _():
        acc_scr[...] = jnp.zeros_like(acc_scr)

    @pl.when(active)
    def _():
        acc_scr[...] += _swiglu_step(xs_ref[...], wg_ref[0], wu_ref[0], wd_ref[0])

    @pl.when(f == pl.num_programs(1) - 1)
    def _():
        ys_ref[...] = acc_scr[...].astype(BF16)


def _moe_combine_kernel(cnt_ref, loff_ref, goff_ref, x_ref, comb_ref, loffv_ref, ys_ref, *refs,
                        tm, S, n_experts, final):
    if final:
        gfin_ref, out_ref, stage_scr, sem = refs
    else:
        out_ref, stage_scr, sem = refs
        gfin_ref = None
    tile = pl.program_id(0)

    @pl.when(tile == 0)
    def _():
        stage_scr[...] = jnp.zeros_like(stage_scr)

    def make_copy(local, glob):
        return pltpu.make_async_copy(ys_ref.at[pl.ds(glob, SEG_ROWS)], stage_scr.at[pl.ds(local, SEG_ROWS)], sem)

    meta = (cnt_ref, loff_ref, goff_ref)
    _segment_copies(meta, tile, n_experts, make_copy, wait=False)

    comb = comb_ref[...]
    lane = lax.broadcasted_iota(jnp.int32, comb.shape, 1).astype(F32)
    slot = loffv_ref[0] + _positions(_selected(comb).astype(BF16), tm)
    cols = lax.broadcasted_iota(jnp.int32, (tm, S), 1).astype(F32)
    g1 = jnp.max(comb, axis=1, keepdims=True)
    i1 = jnp.min(jnp.where(comb == g1, lane, float(LANES)), axis=1, keepdims=True)
    rest = jnp.where(lane == i1, -1.0, comb)
    g2 = jnp.max(rest, axis=1, keepdims=True)
    i2 = jnp.min(jnp.where(rest == g2, lane, float(LANES)), axis=1, keepdims=True)
    picks = []
    for gate, idx in ((g1, i1), (g2, i2)):
        s_k = jnp.sum(jnp.where(lane == idx, slot, 0.0), axis=1, keepdims=True)
        picks.append((gate, jnp.where(cols == s_k, 1.0, 0.0).astype(BF16)))

    _segment_copies(meta, tile, n_experts, make_copy, wait=True)
    stage = stage_scr[...]
    acc = picks[0][0] * _dot(picks[0][1], stage) + picks[1][0] * _dot(picks[1][1], stage)
    _finish(x_ref[...], acc, gfin_ref, out_ref)


def _moe_routed(x, g, wr_hi, wr_lo, wg, wu, wd, gfin=None, *, tm_target=512, block_rows=512, tf_target=512):
    M, D = x.shape
    E, _, F = wg.shape
    tm = _pick_tile(M, tm_target, LANES)
    n_tiles = M // tm
    S = TOP_K * tm + SEG_ROWS * E
    BR = block_rows
    tf = _pick_tile(F, tf_target, LANES)
    n_rows = -(-(TOP_K * M + n_tiles * E * (SEG_ROWS - 1) + E * (BR - 1)) // BR) * BR
    n_blocks = n_rows // BR
    final = gfin is not None

    def tile_spec(n):
        return pl.BlockSpec((tm, n), lambda i, *_: (i, 0))

    def const_spec(a):
        return pl.BlockSpec(a.shape, lambda i, *_: (0,) * a.ndim)

    xn, comb, cnt = pl.pallas_call(
        functools.partial(_moe_route_kernel, n_experts=E), grid=(n_tiles,),
        in_specs=[tile_spec(D), const_spec(g), const_spec(wr_hi), const_spec(wr_lo)],
        out_specs=[tile_spec(D), tile_spec(LANES), pl.BlockSpec((1, 1, LANES), lambda i: (i, 0, 0))],
        out_shape=[jax.ShapeDtypeStruct((M, D), BF16), jax.ShapeDtypeStruct((M, LANES), F32),
                   jax.ShapeDtypeStruct((n_tiles, 1, LANES), F32)],
        compiler_params=_params(("arbitrary",)), name="moe_route")(x, g, wr_hi, wr_lo)

    cnt = cnt[:, 0, :E].astype(jnp.int32)
    cnt_pad = -(-cnt // SEG_ROWS) * SEG_ROWS
    loff = jnp.cumsum(cnt_pad, axis=1) - cnt_pad
    region = -(-jnp.sum(cnt_pad, axis=0) // BR) * BR
    ends = jnp.cumsum(region)
    goff = (ends - region)[None, :] + jnp.cumsum(cnt_pad, axis=0) - cnt_pad
    blk_expert = jnp.minimum(jnp.searchsorted(ends // BR, jnp.arange(n_blocks), side="right"), E - 1).astype(jnp.int32)
    n_active = (ends[-1:] // BR).astype(jnp.int32)
    meta = (cnt_pad.reshape(-1), loff.reshape(-1), goff.reshape(-1))
    loff_vec = jnp.pad(loff.astype(F32), ((0, 0), (0, LANES - E))).reshape(n_tiles, 1, LANES)
    loffv_spec = pl.BlockSpec((1, 1, LANES), lambda i, *_: (i, 0, 0))
    any_spec = pl.BlockSpec(memory_space=pl.ANY)

    xs = pl.pallas_call(
        functools.partial(_moe_dispatch_kernel, tm=tm, S=S, n_experts=E),
        grid_spec=pltpu.PrefetchScalarGridSpec(
            num_scalar_prefetch=3, grid=(n_tiles,),
            in_specs=[tile_spec(D), tile_spec(LANES), loffv_spec, any_spec], out_specs=any_spec,
            scratch_shapes=[pltpu.VMEM((S, D), BF16), pltpu.SemaphoreType.DMA(())]),
        out_shape=jax.ShapeDtypeStruct((n_rows, D), BF16), input_output_aliases={6: 0},
        compiler_params=_params(("arbitrary",)), name="moe_dispatch")(
            *meta, xn, comb, loff_vec, jnp.zeros((n_rows, D), BF16))

    ys = pl.pallas_call(
        _moe_group_kernel,
        grid_spec=pltpu.PrefetchScalarGridSpec(
            num_scalar_prefetch=2, grid=(n_blocks, F // tf),
            in_specs=[pl.BlockSpec((BR, D), lambda i, f, blk, nact: (i, 0)),
                      pl.BlockSpec((1, D, tf), lambda i, f, blk, nact: (blk[i], 0, f)),
                      pl.BlockSpec((1, D, tf), lambda i, f, blk, nact: (blk[i], 0, f)),
                      pl.BlockSpec((1, tf, D), lambda i, f, blk, nact: (blk[i], f, 0))],
            out_specs=pl.BlockSpec((BR, D), lambda i, f, blk, nact: (i, 0)),
            scratch_shapes=[pltpu.VMEM((BR, D), F32)]),
        out_shape=jax.ShapeDtypeStruct((n_rows, D), BF16),
        compiler_params=_params(("arbitrary", "arbitrary")), name="moe_group")(blk_expert, n_active, xs, wg, wu, wd)

    in_specs = [tile_spec(D), tile_spec(LANES), loffv_spec, any_spec]
    args = [x, comb, loff_vec, ys]
    if final:
        in_specs.append(const_spec(gfin))
        args.append(gfin)
    return pl.pallas_call(
        functools.partial(_moe_combine_kernel, tm=tm, S=S, n_experts=E, final=final),
        grid_spec=pltpu.PrefetchScalarGridSpec(
            num_scalar_prefetch=3, grid=(n_tiles,), in_specs=in_specs, out_specs=tile_spec(D),
            scratch_shapes=[pltpu.VMEM((S, D), BF16), pltpu.SemaphoreType.DMA(())]),
        out_shape=jax.ShapeDtypeStruct((M, D), F32),
        compiler_params=_params(("arbitrary",)), name="moe_combine")(*meta, *args)


def kernel(x_prompt, x_sample, cache_k, cache_v, cache_logf, state_conv, page_table, meta_tokens, norm_mix, w_in,
           b_forget, conv_w, w_out_attn, w_out_conv, w_o, norm_ffn, ffn_w_gate, ffn_w_up, ffn_w_down, moe_router,
           moe_w_gate, moe_w_up, moe_w_down, norm_final):
    B, S, D = x_prompt.shape
    NB, T, _ = x_sample.shape
    depth, n_pool, P, H, Dh = cache_k.shape
    n_meta = meta_tokens.shape[0]
    A = H * Dh
    C = conv_w.shape[2]
    halo = conv_w.shape[1] - 1
    E = moe_router.shape[2]
    L = S + n_meta
    Lp = -(-L // 256) * 256
    n_chunks = Lp // LANES
    scale = float(Dh) ** -0.5 * LOG2E

    o_f = 3 * A
    o_c = o_f + H

    def prep_w_in(w):
        wf = jnp.pad(w[:, o_f:o_c], ((0, 0), (0, LANES - H)))
        return jnp.concatenate([w[:, :o_f], w[:, o_c:], wf], axis=1).astype(BF16)

    def row(v):
        return v.reshape(1, -1)

    meta = jnp.broadcast_to(meta_tokens[None].astype(x_prompt.dtype), (B, n_meta, D))
    x_p = jnp.concatenate([meta, x_prompt, jnp.zeros((B, Lp - L, D), x_prompt.dtype)], axis=1).reshape(B * Lp, D)
    x_s = x_sample.reshape(NB * T, D)
    eye_h = jnp.eye(H, dtype=F32)

    outs = {n: [] for n in ("kp", "vp", "lfp", "cp", "ks", "vs", "lfs", "cs")}
    cache_kt = cache_k.transpose(0, 1, 3, 4, 2).reshape(depth, n_pool, A, P)
    cache_vt = cache_v.transpose(0, 1, 3, 4, 2).reshape(depth, n_pool, A, P)
    cache_lft = cache_logf.transpose(0, 1, 3, 2)
    for l in range(depth):
        last = l == depth - 1
        w_l = prep_w_in(w_in[l])
        bf_l = jnp.pad(b_forget[l], (0, LANES - H)).reshape(1, LANES)
        g_l = row(norm_mix[l])
        wa, wc, wo = w_out_attn[l].astype(BF16), w_out_conv[l].astype(BF16), w_o[l].astype(BF16)

        q, k, v, kb, vb, lf, cy, sga, sgc, cst = _inproj(
            x_p, g_l, w_l, bf_l, conv_w[l], A=A, C=C, H=H, scale=scale, seq_len=Lp, valid_len=L)
        outs["kp"].append(k[:, :L])
        outs["vp"].append(v[:, :L])
        lf_chunks = lf.reshape(B, n_chunks, LANES, H).transpose(1, 0, 3, 2).reshape(n_chunks, B * H, LANES)
        fk = _cumsum_chunks(lf_chunks).reshape(n_chunks, B, H, LANES).transpose(1, 2, 0, 3).reshape(B, H, Lp)
        o_a = _attn_prompt(q.reshape(B, Lp, A), kb.reshape(B, Lp, A), vb.reshape(B, Lp, A), fk, H=H, Dh=Dh)
        x_p = _merge(o_a.reshape(B * Lp, A), cy, sga, sgc, x_p, wa, wc, wo)
        outs["lfp"].append(lf.reshape(B, Lp, H)[:, :L])
        outs["cp"].append(cst)

        st = state_conv[l]
        halos = [jnp.pad(st[:, halo - s:], ((0, 0), (0, T - s), (0, 0))).reshape(NB * T, C) for s in range(1, halo + 1)]
        q, k, v, kb, vb, lf, cy, sga, sgc, u = _inproj(
            x_s, g_l, w_l, bf_l, conv_w[l], A=A, C=C, H=H, scale=scale, T=T, halos=halos)
        qbd = (q.reshape(NB, T, H, 1, Dh) * eye_h.astype(BF16)[None, None, :, :, None]).reshape(NB, T * H, A)
        pad_keys = ((0, 0), (0, 0), (0, P - T))
        o_bd = _attn_sample(
            page_table, qbd, jnp.pad(kb.reshape(NB, T, A).transpose(0, 2, 1), pad_keys),
            jnp.pad(vb.reshape(NB, T, A).transpose(0, 2, 1), pad_keys),
            jnp.pad(lf.reshape(NB, T, H).transpose(0, 2, 1), pad_keys),
            cache_kt, cache_vt, cache_lft, layer=l, T=T, H=H)
        o_a = jnp.sum(o_bd.reshape(NB, T, H, H, Dh) * eye_h[None, None, :, :, None], axis=2).reshape(NB * T, A)
        x_s = _merge(o_a.astype(BF16), cy, sga, sgc, x_s, wa, wc, wo)
        outs["ks"].append(k.reshape(NB, T, H, Dh))
        outs["vs"].append(v.reshape(NB, T, H, Dh))
        outs["lfs"].append(lf.reshape(NB, T, H))
        outs["cs"].append(u.reshape(NB, T, C)[:, T - halo:])

        gf = row(norm_ffn[l])
        gfin = row(norm_final) if last else None
        j = l // 2
        if l % 2 == 0:
            wg, wu, wd = ffn_w_gate[j].astype(BF16), ffn_w_up[j].astype(BF16), ffn_w_down[j].astype(BF16)
            x_p = _ffn(x_p, gf, wg, wu, wd, gfin)
            x_s = _ffn(x_s, gf, wg, wu, wd, gfin)
        else:
            wr = jnp.pad(moe_router[j], ((0, 0), (0, LANES - E)))
            wr_hi = wr.astype(BF16)
            wr_lo = (wr - wr_hi.astype(F32)).astype(BF16)
            wg, wu, wd = moe_w_gate[j].astype(BF16), moe_w_up[j].astype(BF16), moe_w_down[j].astype(BF16)
            x_p = _moe_routed(x_p, gf, wr_hi, wr_lo, wg, wu, wd, gfin)
            x_s = _moe_dense(x_s, gf, wr_hi, wr_lo, wg, wu, wd, gfin)

    y_prompt = x_p.reshape(B, Lp, D)[:, n_meta:L]
    y_sample = x_s.reshape(NB, T, D)
    return (y_prompt, y_sample, jnp.stack(outs["kp"]), jnp.stack(outs["vp"]), jnp.stack(outs["lfp"]),
            jnp.stack(outs["cp"]), jnp.stack(outs["ks"]), jnp.stack(outs["vs"]), jnp.stack(outs["lfs"]),
            jnp.stack(outs["cs"]))
```

```python
import functools

import jax
import jax.numpy as jnp
from jax import lax
from jax.experimental import pallas as pl
from jax.experimental.pallas import tpu as pltpu

RMS_EPS = 1e-6
TOP_K = 2
NEG_BIG = -1e30
LOG2E = 1.4426950408889634
LANES = 128
V7X_VMEM_LIMIT_BYTES = 56 * 1024 * 1024

F32 = jnp.float32
BF16 = jnp.bfloat16


def _pick_tile(n, target, mult=8):
    best = None
    for t in range(mult, min(n, target) + 1, mult):
        if n % t == 0:
            best = t
    assert best is not None, (n, target, mult)
    return best


def _params(sem, vmem=V7X_VMEM_LIMIT_BYTES):
    return pltpu.CompilerParams(dimension_semantics=sem, vmem_limit_bytes=vmem)


def _rms(x, g):
    ms = jnp.mean(x * x, axis=-1, keepdims=True)
    return (x * lax.rsqrt(ms + RMS_EPS)) * g


def _log_sigmoid(x):
    return jnp.minimum(x, 0.0) - jnp.log1p(jnp.exp(-jnp.abs(x)))


def _split3(x):
    hi = x.astype(BF16)
    r1 = x - hi.astype(F32)
    mid = r1.astype(BF16)
    lo = (r1 - mid.astype(F32)).astype(BF16)
    return hi, mid, lo


def _dot(a, b):
    return jnp.dot(a, b, preferred_element_type=F32)


def _dot3(x, w_bf16):
    hi, mid, lo = _split3(x)
    return _dot(hi, w_bf16) + _dot(mid, w_bf16) + _dot(lo, w_bf16)


def _inproj_kernel(*refs, tm, A, C, D, H, scale, halo, sample, tiles_per_seq, T, cst_tile, cst_row):
    if sample:
        (x_ref, g_ref, w_ref, bf_ref, cw_ref, *halo_refs) = refs[: 5 + halo]
        (q_ref, k_ref, v_ref, kb_ref, vb_ref, lf_ref, cy_ref, sga_ref, sgc_ref, u_ref) = refs[5 + halo:]
    else:
        (x_ref, g_ref, w_ref, bf_ref, cw_ref,
         q_ref, k_ref, v_ref, kb_ref, vb_ref, lf_ref, cy_ref, sga_ref, sgc_ref, cst_ref, carry_scr) = refs

    xb = _rms(x_ref[...], g_ref[...]).astype(BF16)

    def mm(lo, hi):
        return _dot(xb, w_ref[:, lo:hi])

    zq = mm(0, 3 * A)
    q_ref[...] = (zq[:, :A] * scale).astype(BF16)
    k = zq[:, A:2 * A]
    v = zq[:, 2 * A:3 * A]
    if sample:
        k_ref[...] = k
        v_ref[...] = v
        kb_ref[...] = k.astype(BF16)
    else:
        kt = k.T
        k_ref[0] = kt
        v_ref[0] = v.T
        kb_ref[0] = kt.astype(BF16)
    vb_ref[...] = v.astype(BF16)

    o = 3 * A
    zc = mm(o, o + 3 * C)
    xin, bg, cg = zc[:, :C], zc[:, C:2 * C], zc[:, 2 * C:]
    u = cg * xin
    row = lax.broadcasted_iota(jnp.int32, (tm, C), 0)
    shifted = [u]
    if sample:
        pos = row % T
        for s in range(1, halo + 1):
            shifted.append(jnp.where(pos >= s, pltpu.roll(u, s, 0), halo_refs[s - 1][...]))
        u_ref[...] = u
    else:
        t = pl.program_id(0) % tiles_per_seq

        @pl.when(t == 0)
        def _():
            carry_scr[...] = jnp.zeros_like(carry_scr)

        for s in range(1, halo + 1):
            us = pltpu.roll(u, s, 0)
            for r in range(s):
                us = jnp.where(row == r, carry_scr[halo - s + r:halo - s + r + 1, :], us)
            shifted.append(us)
    y = cw_ref[0:1, :] * shifted[halo]
    for i in range(1, halo + 1):
        y = y + cw_ref[i:i + 1, :] * shifted[halo - i]
    cy_ref[...] = (bg * y).astype(BF16)
    if not sample:
        carry_scr[0:halo, :] = u[tm - halo:tm, :]

        @pl.when(t == cst_tile)
        def _():
            cst_ref[0] = u[cst_row:cst_row + halo, :]

    o += 3 * C
    zg = mm(o, o + 2 * D)
    sga_ref[...] = jax.nn.sigmoid(zg[:, :D])
    sgc_ref[...] = jax.nn.sigmoid(zg[:, D:])

    o += 2 * D
    zf = mm(o, o + LANES)
    lf_ref[...] = _log_sigmoid(zf + bf_ref[...])[:, :H]


def _inproj(x, g, w, bf, cw, *, A, C, H, scale, seq_len=None, valid_len=None, T=None, halos=None, tm_target=384):
    M, D = x.shape
    halo = cw.shape[0] - 1
    sample = halos is not None
    if sample:
        tm = M
        tiles_per_seq, cst_tile, cst_row = 1, 0, 0
        assert T >= halo
    else:
        tm = _pick_tile(seq_len, tm_target, 16)
        tiles_per_seq = seq_len // tm
        cst_tile, cst_row = divmod(valid_len - halo, tm)
        assert cst_row + halo <= tm
    n_seq = M // (tiles_per_seq * tm)
    Nw = w.shape[1]

    def row_spec(n):
        return pl.BlockSpec((tm, n), lambda i: (i, 0))

    def full_spec(a):
        return pl.BlockSpec(a.shape, lambda i: (0,) * a.ndim)

    in_specs = [row_spec(D), full_spec(g), pl.BlockSpec((D, Nw), lambda i: (0, 0), pipeline_mode=pl.Buffered(1)),
                full_spec(bf), full_spec(cw)]
    args = [x, g, w, bf, cw]
    if sample:
        in_specs += [row_spec(C) for _ in halos]
        args += list(halos)
    if sample:
        kv_dims = (M, A)
        kv_spec = row_spec(A)
    else:
        assert tm % LANES == 0
        kv_dims = (n_seq, A, seq_len)
        kv_spec = pl.BlockSpec((1, A, tm), lambda i: (i // tiles_per_seq, 0, i % tiles_per_seq))
    out_shape = [jax.ShapeDtypeStruct((M, A), BF16), jax.ShapeDtypeStruct(kv_dims, F32),
                 jax.ShapeDtypeStruct(kv_dims, F32), jax.ShapeDtypeStruct(kv_dims, BF16),
                 jax.ShapeDtypeStruct((M, A), BF16), jax.ShapeDtypeStruct((M, H), F32),
                 jax.ShapeDtypeStruct((M, C), BF16), jax.ShapeDtypeStruct((M, D), F32), jax.ShapeDtypeStruct((M, D), F32)]
    out_specs = [row_spec(A), kv_spec, kv_spec, kv_spec, row_spec(A), row_spec(H), row_spec(C), row_spec(D),
                 row_spec(D)]
    scratch = []
    if sample:
        out_shape.append(jax.ShapeDtypeStruct((M, C), F32))
        out_specs.append(row_spec(C))
    else:
        out_shape.append(jax.ShapeDtypeStruct((n_seq, halo, C), F32))
        out_specs.append(pl.BlockSpec((1, halo, C), lambda i: (i // tiles_per_seq, 0, 0)))
        scratch.append(pltpu.VMEM((8, C), F32))
    kern = functools.partial(_inproj_kernel, tm=tm, A=A, C=C, D=D, H=H, scale=scale, halo=halo, sample=sample,
                             tiles_per_seq=tiles_per_seq, T=T, cst_tile=cst_tile, cst_row=cst_row)
    return pl.pallas_call(
        kern, grid=(M // tm,), in_specs=in_specs, out_specs=out_specs, out_shape=out_shape,
        scratch_shapes=scratch, compiler_params=_params(("arbitrary",)),
        name="inproj_sample" if sample else "inproj_prompt")(*args)


def _tri_and_ones():
    i = lax.broadcasted_iota(jnp.int32, (LANES, LANES), 0)
    j = lax.broadcasted_iota(jnp.int32, (LANES, LANES), 1)
    return jnp.where(i <= j, 1.0, 0.0).astype(BF16), jnp.ones((LANES, LANES), BF16)


def _cumsum_kernel(x_ref, o_ref, *, n_chunks, R):
    tri, ones = _tri_and_ones()
    x = x_ref[...].reshape(n_chunks * R, LANES)
    hi, mid, lo = _split3(x)
    y = _dot(hi, tri) + _dot(mid, tri) + _dot(lo, tri)
    tot = _dot(hi, ones) + _dot(mid, ones) + _dot(lo, ones)
    carry = jnp.zeros((R, LANES), F32)
    for c in range(n_chunks):
        o_ref[c] = (y[c * R:(c + 1) * R] + carry) * LOG2E
        carry = carry + tot[c * R:(c + 1) * R]


def _cumsum_chunks(x):
    n_chunks, R, _ = x.shape
    return pl.pallas_call(
        functools.partial(_cumsum_kernel, n_chunks=n_chunks, R=R),
        out_shape=jax.ShapeDtypeStruct(x.shape, F32), name="logf_cumsum")(x)


def _attn_kernel(q_ref, kt_ref, v_ref, fk_ref, o_ref, qm_scr, m_scr, l_scr, acc_scr, *, tq, tk, n_super, H, Dh):
    qi = pl.program_id(1)
    r = tq // tk
    n_full = qi * r
    n_sub = tk // LANES
    G = LANES // Dh
    lane = lax.broadcasted_iota(jnp.int32, (tq, LANES), 1)

    def group(h):
        return slice((h // G) * LANES, (h // G + 1) * LANES)

    def own_lanes(h):
        return (lane >= (h % G) * Dh) & (lane < (h % G + 1) * Dh)

    for h in range(H):
        qg = q_ref[0, :, group(h)]
        qm_scr[h] = jnp.where(own_lanes(h), qg, jnp.zeros_like(qg))

    def update(chunks, masked):
        starts = [pl.multiple_of(j * tk, tk) for j in chunks]
        for h in range(H):
            parts = []
            for start in starts:
                s = _dot(qm_scr[h], kt_ref[0, group(h), pl.ds(start, tk)])
                s = s - fk_ref[0, h:h + 1, pl.ds(start, tk)]
                if masked:
                    qpos = qi * tq + lax.broadcasted_iota(jnp.int32, (tq, tk), 0)
                    kpos = start + lax.broadcasted_iota(jnp.int32, (tq, tk), 1)
                    s = jnp.where(kpos <= qpos, s, NEG_BIG)
                parts += [s[:, c * LANES:(c + 1) * LANES] for c in range(n_sub)]
            m_old = m_scr[h]
            row_max = jnp.max(functools.reduce(jnp.maximum, parts), axis=1, keepdims=True)
            m_new = jnp.maximum(m_old, jnp.broadcast_to(row_max, (tq, LANES)))
            alpha = jnp.exp2(m_old - m_new)
            probs = [jnp.exp2(part - m_new) for part in parts]
            l_scr[h] = alpha * l_scr[h] + functools.reduce(jnp.add, probs)
            pv = [_dot(jnp.concatenate(probs[i * n_sub:(i + 1) * n_sub], axis=1).astype(BF16),
                       v_ref[0, pl.ds(start, tk), group(h)]) for i, start in enumerate(starts)]
            acc_scr[h] = alpha * acc_scr[h] + functools.reduce(jnp.add, pv)
            m_scr[h] = m_new

    m_scr[...] = jnp.full_like(m_scr, NEG_BIG)
    l_scr[...] = jnp.zeros_like(l_scr)
    acc_scr[...] = jnp.zeros_like(acc_scr)

    def body(jj, carry):
        update([n_super * jj + i for i in range(n_super)], False)
        return carry

    lax.fori_loop(0, n_full // n_super, body, 0)
    done = (n_full // n_super) * n_super
    size = n_super // 2
    while size >= 1:
        take = (n_full & size) != 0

        @pl.when(take)
        def _(done=done, size=size):
            update([done + i for i in range(size)], False)

        done = done + jnp.where(take, size, 0)
        size //= 2
    for d in range(r):
        update([n_full + d], True)

    for g in range(H // G):
        out = jnp.zeros((tq, LANES), F32)
        for h in range(g * G, (g + 1) * G):
            l = jnp.sum(l_scr[h], axis=1, keepdims=True)
            out = jnp.where(own_lanes(h), acc_scr[h] / l, out)
        o_ref[0, :, group(g * G)] = out.astype(BF16)


def _attn_prompt(q, kt, v, fk, *, H, Dh, tq=256, tk=256, n_super=4):
    B, Lp, A = q.shape
    assert LANES % Dh == 0 and H % (LANES // Dh) == 0 and n_super & (n_super - 1) == 0
    stat = pltpu.VMEM((H, tq, LANES), F32)
    return pl.pallas_call(
        functools.partial(_attn_kernel, tq=tq, tk=tk, n_super=n_super, H=H, Dh=Dh),
        grid=(B, Lp // tq),
        in_specs=[pl.BlockSpec((1, tq, A), lambda b, i: (b, i, 0)),
                  pl.BlockSpec((1, A, Lp), lambda b, i: (b, 0, 0), pipeline_mode=pl.Buffered(1)),
                  pl.BlockSpec((1, Lp, A), lambda b, i: (b, 0, 0), pipeline_mode=pl.Buffered(1)),
                  pl.BlockSpec((1, H, Lp), lambda b, i: (b, 0, 0))],
        out_specs=pl.BlockSpec((1, tq, A), lambda b, i: (b, i, 0)),
        out_shape=jax.ShapeDtypeStruct((B, Lp, A), BF16),
        scratch_shapes=[pltpu.VMEM((H, tq, LANES), BF16), stat, stat, stat],
        compiler_params=_params(("arbitrary", "arbitrary")), name="attn_prompt")(q, kt, v, fk)


def _attn_sample_kernel(pt_ref, qbd_ref, knew_ref, vnew_ref, lfnew_ref, *refs, PP, T, H, P):
    k_refs, v_refs, lf_refs = refs[:PP], refs[PP:2 * PP], refs[2 * PP:3 * PP]
    o_ref, m_scr, l_scr, acc_scr, carry_scr = refs[3 * PP:]
    j = pl.program_id(1)
    R = T * H
    contract_last = (((1,), (1,)), ((), ()))

    @pl.when(j == 0)
    def _():
        m_scr[...] = jnp.full_like(m_scr, NEG_BIG)
        l_scr[...] = jnp.zeros_like(l_scr)
        acc_scr[...] = jnp.zeros_like(acc_scr)
        carry_scr[...] = jnp.zeros_like(carry_scr)

    tri_ones = jnp.concatenate(_tri_and_ones(), axis=1)
    qbd = qbd_ref[0]

    def pages(state, lfs, kts, vts, mask):
        m, l, acc, carry = state
        n = len(lfs)
        pad = (-H) % 16
        Hp = H + pad
        zeros = [jnp.zeros((pad, P), F32)] if pad else []
        terms = _split3(jnp.concatenate([piece for lf in lfs for piece in [lf] + zeros], axis=0))
        y = _dot(jnp.concatenate(terms, axis=0), tri_ones)
        y = y[0:n * Hp] + y[n * Hp:2 * n * Hp] + y[2 * n * Hp:3 * n * Hp]
        ss = []
        for i in range(n):
            fk = carry + y[i * Hp:i * Hp + H, :P]
            carry = carry + y[i * Hp:i * Hp + H, P:]
            s = _dot(qbd, kts[i]) - jnp.concatenate([fk * LOG2E] * T, axis=0)
            ss.append(s if mask is None else jnp.where(mask, s, NEG_BIG))
        m_new = jnp.maximum(m, jnp.max(functools.reduce(jnp.maximum, ss), axis=1, keepdims=True))
        alpha = jnp.exp2(m - m_new)
        ps = [jnp.exp2(s - m_new) for s in ss]
        l = alpha * l + jnp.sum(functools.reduce(jnp.add, ps), axis=1, keepdims=True)
        pv = [lax.dot_general(p.astype(BF16), vt, contract_last, preferred_element_type=F32)
              for p, vt in zip(ps, vts)]
        acc = alpha * acc + functools.reduce(jnp.add, pv)
        return m_new, l, acc, carry

    state = (m_scr[...], l_scr[...], acc_scr[...], carry_scr[...])
    state = pages(state, [r[0, 0] for r in lf_refs], [r[0, 0].astype(BF16) for r in k_refs],
                  [r[0, 0].astype(BF16) for r in v_refs], None)
    m_scr[...], l_scr[...], acc_scr[...], carry_scr[...] = state

    @pl.when(j == pl.num_programs(1) - 1)
    def _():
        t_row = lax.broadcasted_iota(jnp.int32, (R, P), 0) // H
        key = lax.broadcasted_iota(jnp.int32, (R, P), 1)
        _, l, acc, _ = pages(state, [lfnew_ref[0]], [knew_ref[0]], [vnew_ref[0]], key <= t_row)
        o_ref[0] = acc / l


def _attn_sample(page_table, qbd, k_new, v_new, lf_new, cache_k, cache_v, cache_lf, *, layer, T, H, pages_per_step=16):
    n_seq, n_pages = page_table.shape
    _, _, A, P = cache_k.shape
    R = T * H
    PP = _pick_tile(n_pages, pages_per_step, 1)

    def seq_spec(shape):
        return pl.BlockSpec((1,) + shape, lambda s, j, pt: (s, 0, 0))

    def page_spec(shape, p):
        return pl.BlockSpec((1, 1) + shape, lambda s, j, pt: (layer, pt[s, j * PP + p], 0, 0))

    in_specs = ([seq_spec((R, A)), seq_spec((A, P)), seq_spec((A, P)), seq_spec((H, P))]
                + [page_spec((A, P), p) for p in range(PP)] + [page_spec((A, P), p) for p in range(PP)]
                + [page_spec((H, P), p) for p in range(PP)])
    grid_spec = pltpu.PrefetchScalarGridSpec(
        num_scalar_prefetch=1, grid=(n_seq, n_pages // PP), in_specs=in_specs,
        out_specs=pl.BlockSpec((1, R, A), lambda s, j, pt: (s, 0, 0)),
        scratch_shapes=[pltpu.VMEM((R, 1), F32), pltpu.VMEM((R, 1), F32), pltpu.VMEM((R, A), F32),
                        pltpu.VMEM((H, P), F32)])
    return pl.pallas_call(
        functools.partial(_attn_sample_kernel, PP=PP, T=T, H=H, P=P), grid_spec=grid_spec,
        out_shape=jax.ShapeDtypeStruct((n_seq, R, A), F32),
        compiler_params=_params(("arbitrary", "arbitrary")), name="attn_sample")(
            page_table, qbd, k_new, v_new, lf_new, *([cache_k] * PP), *([cache_v] * PP), *([cache_lf] * PP))


def _merge_kernel(o_ref, cy_ref, sga_ref, sgc_ref, x_ref, wa_ref, wc_ref, wo_ref, out_ref):
    ya = _dot(o_ref[...], wa_ref[...])
    yc = _dot(cy_ref[...], wc_ref[...])
    mix = (sga_ref[...] * ya + sgc_ref[...] * yc).astype(BF16)
    out_ref[...] = x_ref[...] + _dot(mix, wo_ref[...])


def _merge(o, cy, sga, sgc, x, wa, wc, wo, *, tm_target=512):
    M, D = x.shape
    tm = _pick_tile(M, tm_target, 16)

    def row_spec(n):
        return pl.BlockSpec((tm, n), lambda i: (i, 0))

    def w_spec(a):
        return pl.BlockSpec(a.shape, lambda i: (0, 0))

    return pl.pallas_call(
        _merge_kernel, grid=(M // tm,),
        in_specs=[row_spec(o.shape[1]), row_spec(cy.shape[1]), row_spec(D), row_spec(D), row_spec(D),
                  w_spec(wa), w_spec(wc), w_spec(wo)],
        out_specs=row_spec(D), out_shape=jax.ShapeDtypeStruct((M, D), F32),
        compiler_params=_params(("arbitrary",)), name="merge")(o, cy, sga, sgc, x, wa, wc, wo)


def _swiglu_step(xb, wg, wu, wd, row_scale=None):
    gate = _dot(xb, wg)
    up = _dot(xb, wu)
    h = jax.nn.silu(gate) * up
    if row_scale is not None:
        h = h * row_scale
    return _dot(h.astype(BF16), wd)


def _finish(x, acc, gfin_ref, out_ref):
    y = x + acc
    if gfin_ref is not None:
        y = _rms(y, gfin_ref[...])
    out_ref[...] = y


def _ffn_kernel(*refs, final):
    if final:
        x_ref, g_ref, wg_ref, wu_ref, wd_ref, gfin_ref, out_ref, xn_scr, acc_scr = refs
    else:
        x_ref, g_ref, wg_ref, wu_ref, wd_ref, out_ref, xn_scr, acc_scr = refs
        gfin_ref = None
    f = pl.program_id(1)

    @pl.when(f == 0)
    def _():
        xn_scr[...] = _rms(x_ref[...], g_ref[...]).astype(BF16)
        acc_scr[...] = jnp.zeros_like(acc_scr)

    acc_scr[...] += _swiglu_step(xn_scr[...], wg_ref[...], wu_ref[...], wd_ref[...])

    @pl.when(f == pl.num_programs(1) - 1)
    def _():
        _finish(x_ref[...], acc_scr[...], gfin_ref, out_ref)


def _ffn(x, g, wg, wu, wd, gfin=None, *, tm_target=1024, tf_target=512):
    M, D = x.shape
    F = wg.shape[1]
    tm = _pick_tile(M, tm_target, 16)
    tf = _pick_tile(F, tf_target, LANES)
    final = gfin is not None
    in_specs = [pl.BlockSpec((tm, D), lambda i, f: (i, 0)), pl.BlockSpec((1, D), lambda i, f: (0, 0)),
                pl.BlockSpec((D, tf), lambda i, f: (0, f)), pl.BlockSpec((D, tf), lambda i, f: (0, f)),
                pl.BlockSpec((tf, D), lambda i, f: (f, 0))]
    args = [x, g, wg, wu, wd]
    if final:
        in_specs.append(pl.BlockSpec((1, D), lambda i, f: (0, 0)))
        args.append(gfin)
    return pl.pallas_call(
        functools.partial(_ffn_kernel, final=final), grid=(M // tm, F // tf), in_specs=in_specs,
        out_specs=pl.BlockSpec((tm, D), lambda i, f: (i, 0)), out_shape=jax.ShapeDtypeStruct((M, D), F32),
        scratch_shapes=[pltpu.VMEM((tm, D), BF16), pltpu.VMEM((tm, D), F32)],
        compiler_params=_params(("arbitrary", "arbitrary")), name="ffn_dense")(*args)


def _route(xn, wr_hi, wr_lo, n_experts):
    hi = xn.astype(BF16)
    lo = (xn - hi.astype(F32)).astype(BF16)
    logits = _dot(hi, wr_hi) + _dot(lo, wr_hi) + _dot(hi, wr_lo)
    lane = lax.broadcasted_iota(jnp.int32, logits.shape, 1).astype(F32)
    lg = jnp.where(lane < n_experts, logits, NEG_BIG)
    v1 = jnp.max(lg, axis=1, keepdims=True)
    i1 = jnp.min(jnp.where(lg == v1, lane, float(LANES)), axis=1, keepdims=True)
    lg2 = jnp.where(lane == i1, NEG_BIG, lg)
    v2 = jnp.max(lg2, axis=1, keepdims=True)
    i2 = jnp.min(jnp.where(lg2 == v2, lane, float(LANES)), axis=1, keepdims=True)
    e2 = jnp.exp(v2 - v1)
    den = 1.0 + e2
    return jnp.where(lane == i1, 1.0 / den, 0.0) + jnp.where(lane == i2, e2 / den, 0.0)


def _moe_dense_kernel(*refs, final, n_experts):
    if final:
        x_ref, g_ref, wrh_ref, wrl_ref, wg_ref, wu_ref, wd_ref, gfin_ref, out_ref, xn_scr, acc_scr, comb_scr = refs
    else:
        x_ref, g_ref, wrh_ref, wrl_ref, wg_ref, wu_ref, wd_ref, out_ref, xn_scr, acc_scr, comb_scr = refs
        gfin_ref = None
    e = pl.program_id(1)
    f = pl.program_id(2)

    @pl.when((e == 0) & (f == 0))
    def _():
        xn = _rms(x_ref[...], g_ref[...])
        xn_scr[...] = xn.astype(BF16)
        comb_scr[...] = _route(xn, wrh_ref[...], wrl_ref[...], n_experts)
        acc_scr[...] = jnp.zeros_like(acc_scr)

    comb = comb_scr[...]
    lane = lax.broadcasted_iota(jnp.int32, comb.shape, 1)
    ce = jnp.sum(jnp.where(lane == e, comb, 0.0), axis=1, keepdims=True)
    acc_scr[...] += _swiglu_step(xn_scr[...], wg_ref[0], wu_ref[0], wd_ref[0], row_scale=ce)

    @pl.when((e == pl.num_programs(1) - 1) & (f == pl.num_programs(2) - 1))
    def _():
        _finish(x_ref[...], acc_scr[...], gfin_ref, out_ref)


def _moe_dense(x, g, wr_hi, wr_lo, wg, wu, wd, gfin=None, *, tm_target=1024, tf_target=512):
    M, D = x.shape
    E, _, F = wg.shape
    tm = _pick_tile(M, tm_target, 16)
    tf = _pick_tile(F, tf_target, LANES)
    final = gfin is not None
    vec = pl.BlockSpec((1, D), lambda i, e, f: (0, 0))
    wr_spec = pl.BlockSpec((D, LANES), lambda i, e, f: (0, 0))
    in_specs = [pl.BlockSpec((tm, D), lambda i, e, f: (i, 0)), vec, wr_spec, wr_spec,
                pl.BlockSpec((1, D, tf), lambda i, e, f: (e, 0, f)), pl.BlockSpec((1, D, tf), lambda i, e, f: (e, 0, f)),
                pl.BlockSpec((1, tf, D), lambda i, e, f: (e, f, 0))]
    args = [x, g, wr_hi, wr_lo, wg, wu, wd]
    if final:
        in_specs.append(vec)
        args.append(gfin)
    return pl.pallas_call(
        functools.partial(_moe_dense_kernel, final=final, n_experts=E), grid=(M // tm, E, F // tf),
        in_specs=in_specs, out_specs=pl.BlockSpec((tm, D), lambda i, e, f: (i, 0)),
        out_shape=jax.ShapeDtypeStruct((M, D), F32),
        scratch_shapes=[pltpu.VMEM((tm, D), BF16), pltpu.VMEM((tm, D), F32), pltpu.VMEM((tm, LANES), F32)],
        compiler_params=_params(("arbitrary", "arbitrary", "arbitrary")), name="moe_dense")(*args)


SEG_ROWS = 16


def _selected(comb):
    return jnp.where(comb > 0.0, 1.0, 0.0)


def _positions(sel_bf16, tm):
    r = lax.broadcasted_iota(jnp.int32, (tm, tm), 0)
    c = lax.broadcasted_iota(jnp.int32, (tm, tm), 1)
    return _dot(jnp.where(c < r, 1.0, 0.0).astype(BF16), sel_bf16)


def _moe_route_kernel(x_ref, g_ref, wrh_ref, wrl_ref, xn_ref, comb_ref, cnt_ref, *, tm, n_experts, seq_len, valid_len):
    xn = _rms(x_ref[...], g_ref[...])
    xn_ref[...] = xn.astype(BF16)
    comb = _route(xn, wrh_ref[...], wrl_ref[...], n_experts)
    row = pl.program_id(0) * tm + lax.broadcasted_iota(jnp.int32, (tm, 1), 0)
    comb = jnp.where(row % seq_len < valid_len, comb, 0.0)
    comb_ref[...] = comb
    cnt_ref[0] = jnp.sum(_selected(comb), axis=0, keepdims=True)


def _segment_copies(meta, tile, n_experts, make_copy, wait):
    cnt_ref, loff_ref, goff_ref = meta
    for e in range(n_experts):
        idx = tile * n_experts + e
        local0, global0 = loff_ref[idx], goff_ref[idx]

        def body(k, carry, local0=local0, global0=global0):
            cp = make_copy(pl.multiple_of(local0 + k * SEG_ROWS, SEG_ROWS),
                           pl.multiple_of(global0 + k * SEG_ROWS, SEG_ROWS))
            if wait:
                cp.wait()
            else:
                cp.start()
            return carry

        lax.fori_loop(0, cnt_ref[idx] // SEG_ROWS, body, 0)


def _moe_dispatch_kernel(cnt_ref, loff_ref, goff_ref, xn_ref, comb_ref, loffv_ref, xs_in_ref, xs_ref,
                         stage_scr, sem, *, tm, S, n_experts):
    del xs_in_ref
    tile = pl.program_id(0)
    sel = _selected(comb_ref[...])
    slot = loffv_ref[0] + _positions(sel.astype(BF16), tm)
    slot_t = jnp.where(sel > 0.0, slot, -1.0).T
    rows = lax.broadcasted_iota(jnp.int32, (S, tm), 0).astype(F32)
    onehot = jnp.zeros((S, tm), F32)
    for e in range(n_experts):
        onehot = onehot + jnp.where(slot_t[e:e + 1, :] == rows, 1.0, 0.0)
    stage_scr[...] = _dot(onehot.astype(BF16), xn_ref[...]).astype(BF16)

    def make_copy(local, glob):
        return pltpu.make_async_copy(stage_scr.at[pl.ds(local, SEG_ROWS)], xs_ref.at[pl.ds(glob, SEG_ROWS)], sem)

    meta = (cnt_ref, loff_ref, goff_ref)
    _segment_copies(meta, tile, n_experts, make_copy, wait=False)
    _segment_copies(meta, tile, n_experts, make_copy, wait=True)


def _moe_group_kernel(blk_ref, nact_ref, xs_ref, wg_ref, wu_ref, wd_ref, ys_ref, acc_scr):
    del blk_ref
    i = pl.program_id(0)
    f = pl.program_id(1)
    active = i < nact_ref[0]

    @pl.when(f == 0)
    def _():
        acc_scr[...] = jnp.zeros_like(acc_scr)

    @pl.when(active)
    def _():
        acc_scr[...] += _swiglu_step(xs_ref[...], wg_ref[0], wu_ref[0], wd_ref[0])

    @pl.when(f == pl.num_programs(1) - 1)
    def _():
        ys_ref[...] = acc_scr[...].astype(BF16)


def _moe_combine_kernel(cnt_ref, loff_ref, goff_ref, x_ref, comb_ref, loffv_ref, ys_ref, *refs,
                        tm, S, n_experts, final):
    if final:
        gfin_ref, out_ref, stage_scr, sem = refs
    else:
        out_ref, stage_scr, sem = refs
        gfin_ref = None
    tile = pl.program_id(0)

    @pl.when(tile == 0)
    def _():
        stage_scr[...] = jnp.zeros_like(stage_scr)

    def make_copy(local, glob):
        return pltpu.make_async_copy(ys_ref.at[pl.ds(glob, SEG_ROWS)], stage_scr.at[pl.ds(local, SEG_ROWS)], sem)

    meta = (cnt_ref, loff_ref, goff_ref)
    _segment_copies(meta, tile, n_experts, make_copy, wait=False)

    comb = comb_ref[...]
    lane = lax.broadcasted_iota(jnp.int32, comb.shape, 1).astype(F32)
    slot = loffv_ref[0] + _positions(_selected(comb).astype(BF16), tm)
    cols = lax.broadcasted_iota(jnp.int32, (tm, S), 1).astype(F32)
    g1 = jnp.max(comb, axis=1, keepdims=True)
    i1 = jnp.min(jnp.where(comb == g1, lane, float(LANES)), axis=1, keepdims=True)
    rest = jnp.where(lane == i1, -1.0, comb)
    g2 = jnp.max(rest, axis=1, keepdims=True)
    i2 = jnp.min(jnp.where(rest == g2, lane, float(LANES)), axis=1, keepdims=True)
    picks = []
    for gate, idx in ((g1, i1), (g2, i2)):
        s_k = jnp.sum(jnp.where(lane == idx, slot, 0.0), axis=1, keepdims=True)
        picks.append((gate, jnp.where(cols == s_k, 1.0, 0.0).astype(BF16)))

    _segment_copies(meta, tile, n_experts, make_copy, wait=True)
    stage = stage_scr[...]
    acc = picks[0][0] * _dot(picks[0][1], stage) + picks[1][0] * _dot(picks[1][1], stage)
    _finish(x_ref[...], acc, gfin_ref, out_ref)


def _moe_routed(x, g, wr_hi, wr_lo, wg, wu, wd, gfin=None, *, seq_len, valid_len, tm_target=512, block_rows=1024,
                tf_target=512):
    M, D = x.shape
    E, _, F = wg.shape
    tm = _pick_tile(M, tm_target, LANES)
    n_tiles = M // tm
    S = TOP_K * tm + SEG_ROWS * E
    BR = block_rows
    tf = _pick_tile(F, tf_target, LANES)
    n_rows = -(-(TOP_K * M + n_tiles * E * (SEG_ROWS - 1) + E * (BR - 1)) // BR) * BR
    n_blocks = n_rows // BR
    final = gfin is not None

    def tile_spec(n):
        return pl.BlockSpec((tm, n), lambda i, *_: (i, 0))

    def const_spec(a):
        return pl.BlockSpec(a.shape, lambda i, *_: (0,) * a.ndim)

    xn, comb, cnt = pl.pallas_call(
        functools.partial(_moe_route_kernel, tm=tm, n_experts=E, seq_len=seq_len, valid_len=valid_len),
        grid=(n_tiles,),
        in_specs=[tile_spec(D), const_spec(g), const_spec(wr_hi), const_spec(wr_lo)],
        out_specs=[tile_spec(D), tile_spec(LANES), pl.BlockSpec((1, 1, LANES), lambda i: (i, 0, 0))],
        out_shape=[jax.ShapeDtypeStruct((M, D), BF16), jax.ShapeDtypeStruct((M, LANES), F32),
                   jax.ShapeDtypeStruct((n_tiles, 1, LANES), F32)],
        compiler_params=_params(("arbitrary",)), name="moe_route")(x, g, wr_hi, wr_lo)

    cnt = cnt[:, 0, :E].astype(jnp.int32)
    cnt_pad = -(-cnt // SEG_ROWS) * SEG_ROWS
    loff = jnp.cumsum(cnt_pad, axis=1) - cnt_pad
    region = -(-jnp.sum(cnt_pad, axis=0) // BR) * BR
    ends = jnp.cumsum(region)
    goff = (ends - region)[None, :] + jnp.cumsum(cnt_pad, axis=0) - cnt_pad
    blk_expert = jnp.minimum(jnp.searchsorted(ends // BR, jnp.arange(n_blocks), side="right"), E - 1).astype(jnp.int32)
    n_active = (ends[-1:] // BR).astype(jnp.int32)
    meta = (cnt_pad.reshape(-1), loff.reshape(-1), goff.reshape(-1))
    loff_vec = jnp.pad(loff.astype(F32), ((0, 0), (0, LANES - E))).reshape(n_tiles, 1, LANES)
    loffv_spec = pl.BlockSpec((1, 1, LANES), lambda i, *_: (i, 0, 0))
    any_spec = pl.BlockSpec(memory_space=pl.ANY)

    xs = pl.pallas_call(
        functools.partial(_moe_dispatch_kernel, tm=tm, S=S, n_experts=E),
        grid_spec=pltpu.PrefetchScalarGridSpec(
            num_scalar_prefetch=3, grid=(n_tiles,),
            in_specs=[tile_spec(D), tile_spec(LANES), loffv_spec, any_spec], out_specs=any_spec,
            scratch_shapes=[pltpu.VMEM((S, D), BF16), pltpu.SemaphoreType.DMA(())]),
        out_shape=jax.ShapeDtypeStruct((n_rows, D), BF16), input_output_aliases={6: 0},
        compiler_params=_params(("arbitrary",)), name="moe_dispatch")(
            *meta, xn, comb, loff_vec, jnp.zeros((n_rows, D), BF16))

    ys = pl.pallas_call(
        _moe_group_kernel,
        grid_spec=pltpu.PrefetchScalarGridSpec(
            num_scalar_prefetch=2, grid=(n_blocks, F // tf),
            in_specs=[pl.BlockSpec((BR, D), lambda i, f, blk, nact: (i, 0)),
                      pl.BlockSpec((1, D, tf), lambda i, f, blk, nact: (blk[i], 0, f)),
                      pl.BlockSpec((1, D, tf), lambda i, f, blk, nact: (blk[i], 0, f)),
                      pl.BlockSpec((1, tf, D), lambda i, f, blk, nact: (blk[i], f, 0))],
            out_specs=pl.BlockSpec((BR, D), lambda i, f, blk, nact: (i, 0)),
            scratch_shapes=[pltpu.VMEM((BR, D), F32)]),
        out_shape=jax.ShapeDtypeStruct((n_rows, D), BF16),
        compiler_params=_params(("arbitrary", "arbitrary")), name="moe_group")(blk_expert, n_active, xs, wg, wu, wd)

    in_specs = [tile_spec(D), tile_spec(LANES), loffv_spec, any_spec]
    args = [x, comb, loff_vec, ys]
    if final:
        in_specs.append(const_spec(gfin))
        args.append(gfin)
    return pl.pallas_call(
        functools.partial(_moe_combine_kernel, tm=tm, S=S, n_experts=E, final=final),
        grid_spec=pltpu.PrefetchScalarGridSpec(
            num_scalar_prefetch=3, grid=(n_tiles,), in_specs=in_specs, out_specs=tile_spec(D),
            scratch_shapes=[pltpu.VMEM((S, D), BF16), pltpu.SemaphoreType.DMA(())]),
        out_shape=jax.ShapeDtypeStruct((M, D), F32),
        compiler_params=_params(("arbitrary",)), name="moe_combine")(*meta, *args)


def kernel(x_prompt, x_sample, cache_k, cache_v, cache_logf, state_conv, page_table, meta_tokens, norm_mix, w_in,
           b_forget, conv_w, w_out_attn, w_out_conv, w_o, norm_ffn, ffn_w_gate, ffn_w_up, ffn_w_down, moe_router,
           moe_w_gate, moe_w_up, moe_w_down, norm_final):
    B, S, D = x_prompt.shape
    NB, T, _ = x_sample.shape
    depth, n_pool, P, H, Dh = cache_k.shape
    n_meta = meta_tokens.shape[0]
    A = H * Dh
    C = conv_w.shape[2]
    halo = conv_w.shape[1] - 1
    E = moe_router.shape[2]
    L = S + n_meta
    Lp = -(-L // 256) * 256
    n_chunks = Lp // LANES
    scale = float(Dh) ** -0.5 * LOG2E

    o_f = 3 * A
    o_c = o_f + H

    def prep_w_in(w):
        wf = jnp.pad(w[:, o_f:o_c], ((0, 0), (0, LANES - H)))
        return jnp.concatenate([w[:, :o_f], w[:, o_c:], wf], axis=1).astype(BF16)

    def row(v):
        return v.reshape(1, -1)

    meta = jnp.broadcast_to(meta_tokens[None].astype(x_prompt.dtype), (B, n_meta, D))
    x_p = jnp.concatenate([meta, x_prompt, jnp.zeros((B, Lp - L, D), x_prompt.dtype)], axis=1).reshape(B * Lp, D)
    x_s = x_sample.reshape(NB * T, D)
    eye_h = jnp.eye(H, dtype=F32)

    outs = {n: [] for n in ("kp", "vp", "lfp", "cp", "ks", "vs", "lfs", "cs")}
    cache_kt = cache_k.transpose(0, 1, 3, 4, 2).reshape(depth, n_pool, A, P)
    cache_vt = cache_v.transpose(0, 1, 3, 4, 2).reshape(depth, n_pool, A, P)
    cache_lft = cache_logf.transpose(0, 1, 3, 2)
    for l in range(depth):
        last = l == depth - 1
        w_l = prep_w_in(w_in[l])
        bf_l = jnp.pad(b_forget[l], (0, LANES - H)).reshape(1, LANES)
        g_l = row(norm_mix[l])
        wa, wc, wo = w_out_attn[l].astype(BF16), w_out_conv[l].astype(BF16), w_o[l].astype(BF16)

        q, kt, vt, ktb, vb, lf, cy, sga, sgc, cst = _inproj(
            x_p, g_l, w_l, bf_l, conv_w[l], A=A, C=C, H=H, scale=scale, seq_len=Lp, valid_len=L)
        outs["kp"].append(kt[:, :, :L].reshape(B, H, Dh, L))
        outs["vp"].append(vt[:, :, :L].reshape(B, H, Dh, L))
        lf_chunks = lf.reshape(B, n_chunks, LANES, H).transpose(1, 0, 3, 2).reshape(n_chunks, B * H, LANES)
        fk = _cumsum_chunks(lf_chunks).reshape(n_chunks, B, H, LANES).transpose(1, 2, 0, 3).reshape(B, H, Lp)
        o_a = _attn_prompt(q.reshape(B, Lp, A), ktb, vb.reshape(B, Lp, A), fk, H=H, Dh=Dh)
        x_p = _merge(o_a.reshape(B * Lp, A), cy, sga, sgc, x_p, wa, wc, wo)
        outs["lfp"].append(lf.reshape(B, Lp, H)[:, :L])
        outs["cp"].append(cst)

        st = state_conv[l]
        halos = [jnp.pad(st[:, halo - s:], ((0, 0), (0, T - s), (0, 0))).reshape(NB * T, C) for s in range(1, halo + 1)]
        q, k, v, kb, vb, lf, cy, sga, sgc, u = _inproj(
            x_s, g_l, w_l, bf_l, conv_w[l], A=A, C=C, H=H, scale=scale, T=T, halos=halos)
        qbd = (q.reshape(NB, T, H, 1, Dh) * eye_h.astype(BF16)[None, None, :, :, None]).reshape(NB, T * H, A)
        pad_keys = ((0, 0), (0, 0), (0, P - T))
        o_bd = _attn_sample(
            page_table, qbd, jnp.pad(kb.reshape(NB, T, A).transpose(0, 2, 1), pad_keys),
            jnp.pad(vb.reshape(NB, T, A).transpose(0, 2, 1), pad_keys),
            jnp.pad(lf.reshape(NB, T, H).transpose(0, 2, 1), pad_keys),
            cache_kt, cache_vt, cache_lft, layer=l, T=T, H=H)
        o_a = jnp.sum(o_bd.reshape(NB, T, H, H, Dh) * eye_h[None, None, :, :, None], axis=2).reshape(NB * T, A)
        x_s = _merge(o_a.astype(BF16), cy, sga, sgc, x_s, wa, wc, wo)
        outs["ks"].append(k.reshape(NB, T, H, Dh))
        outs["vs"].append(v.reshape(NB, T, H, Dh))
        outs["lfs"].append(lf.reshape(NB, T, H))
        outs["cs"].append(u.reshape(NB, T, C)[:, T - halo:])

        gf = row(norm_ffn[l])
        gfin = row(norm_final) if last else None
        j = l // 2
        if l % 2 == 0:
            wg, wu, wd = ffn_w_gate[j].astype(BF16), ffn_w_up[j].astype(BF16), ffn_w_down[j].astype(BF16)
            x_p = _ffn(x_p, gf, wg, wu, wd, gfin)
            x_s = _ffn(x_s, gf, wg, wu, wd, gfin)
        else:
            wr = jnp.pad(moe_router[j], ((0, 0), (0, LANES - E)))
            wr_hi = wr.astype(BF16)
            wr_lo = (wr - wr_hi.astype(F32)).astype(BF16)
            wg, wu, wd = moe_w_gate[j].astype(BF16), moe_w_up[j].astype(BF16), moe_w_down[j].astype(BF16)
            x_p = _moe_routed(x_p, gf, wr_hi, wr_lo, wg, wu, wd, gfin, seq_len=Lp, valid_len=L)
            x_s = _moe_dense(x_s, gf, wr_hi, wr_lo, wg, wu, wd, gfin)

    y_prompt = x_p.reshape(B, Lp, D)[:, n_meta:L]
    y_sample = x_s.reshape(NB, T, D)
    k_prompt = jnp.stack(outs["kp"]).transpose(0, 1, 4, 2, 3)
    v_prompt = jnp.stack(outs["vp"]).transpose(0, 1, 4, 2, 3)
    return (y_prompt, y_sample, k_prompt, v_prompt, jnp.stack(outs["lfp"]),
            jnp.stack(outs["cp"]), jnp.stack(outs["ks"]), jnp.stack(outs["vs"]), jnp.stack(outs["lfs"]),
            jnp.stack(outs["cs"]))
```

```python
import functools

import jax
import jax.numpy as jnp
from jax import lax
from jax.experimental import pallas as pl
from jax.experimental.pallas import tpu as pltpu

RMS_EPS = 1e-6
TOP_K = 2
NEG_BIG = -1e30
LOG2E = 1.4426950408889634
LANES = 128
V7X_VMEM_LIMIT_BYTES = 56 * 1024 * 1024

F32 = jnp.float32
BF16 = jnp.bfloat16


def _pick_tile(n, target, mult=8):
    best = None
    for t in range(mult, min(n, target) + 1, mult):
        if n % t == 0:
            best = t
    assert best is not None, (n, target, mult)
    return best


def _params(sem, vmem=V7X_VMEM_LIMIT_BYTES):
    return pltpu.CompilerParams(dimension_semantics=sem, vmem_limit_bytes=vmem)


def _rms(x, g):
    ms = jnp.mean(x * x, axis=-1, keepdims=True)
    return (x * lax.rsqrt(ms + RMS_EPS)) * g


def _log_sigmoid(x):
    return jnp.minimum(x, 0.0) - jnp.log1p(jnp.exp(-jnp.abs(x)))


def _split3(x):
    hi = x.astype(BF16)
    r1 = x - hi.astype(F32)
    mid = r1.astype(BF16)
    lo = (r1 - mid.astype(F32)).astype(BF16)
    return hi, mid, lo


def _dot(a, b):
    return jnp.dot(a, b, preferred_element_type=F32)


def _dot3(x, w_bf16):
    hi, mid, lo = _split3(x)
    return _dot(hi, w_bf16) + _dot(mid, w_bf16) + _dot(lo, w_bf16)


def _inproj_kernel(*refs, tm, A, C, D, H, scale, halo, sample, tiles_per_seq, T, cst_tile, cst_row):
    if sample:
        (x_ref, g_ref, w_ref, bf_ref, cw_ref, *halo_refs) = refs[: 5 + halo]
        (q_ref, k_ref, v_ref, kb_ref, vb_ref, lf_ref, cy_ref, sga_ref, sgc_ref, u_ref) = refs[5 + halo:]
    else:
        (x_ref, g_ref, w_ref, bf_ref, cw_ref,
         q_ref, k_ref, v_ref, kb_ref, vb_ref, lf_ref, cy_ref, sga_ref, sgc_ref, cst_ref, stats_ref,
         carry_scr) = refs

    xb = _rms(x_ref[...], g_ref[...]).astype(BF16)

    def mm(lo, hi):
        return _dot(xb, w_ref[:, lo:hi])

    zq = mm(0, 3 * A)
    qb = (zq[:, :A] * scale).astype(BF16)
    q_ref[...] = qb
    k = zq[:, A:2 * A]
    v = zq[:, 2 * A:3 * A]
    if sample:
        k_ref[...] = k
        v_ref[...] = v
        kb_ref[...] = k.astype(BF16)
    else:
        kt = k.T
        k_ref[0] = kt
        v_ref[0] = v.T
        kb_ref[0] = kt.astype(BF16)
        qf = qb.astype(F32)
        kf = k.astype(BF16).astype(F32)
        feat = lax.broadcasted_iota(jnp.int32, (A, LANES), 0) // (A // H)
        col = lax.broadcasted_iota(jnp.int32, (A, LANES), 1)
        stats = jnp.zeros((tm, LANES), F32)
        for i, prod in enumerate((qf * qf, kf * kf, qf * kf)):
            stats = stats + _dot(prod.astype(BF16), jnp.where(col == feat + i * H, 1.0, 0.0).astype(BF16))
        stats_ref[...] = stats
    vb_ref[...] = v.astype(BF16)

    o = 3 * A
    zc = mm(o, o + 3 * C)
    xin, bg, cg = zc[:, :C], zc[:, C:2 * C], zc[:, 2 * C:]
    u = cg * xin
    row = lax.broadcasted_iota(jnp.int32, (tm, C), 0)
    shifted = [u]
    if sample:
        pos = row % T
        for s in range(1, halo + 1):
            shifted.append(jnp.where(pos >= s, pltpu.roll(u, s, 0), halo_refs[s - 1][...]))
        u_ref[...] = u
    else:
        t = pl.program_id(0) % tiles_per_seq

        @pl.when(t == 0)
        def _():
            carry_scr[...] = jnp.zeros_like(carry_scr)

        for s in range(1, halo + 1):
            us = pltpu.roll(u, s, 0)
            for r in range(s):
                us = jnp.where(row == r, carry_scr[halo - s + r:halo - s + r + 1, :], us)
            shifted.append(us)
    y = cw_ref[0:1, :] * shifted[halo]
    for i in range(1, halo + 1):
        y = y + cw_ref[i:i + 1, :] * shifted[halo - i]
    cy_ref[...] = (bg * y).astype(BF16)
    if not sample:
        carry_scr[0:halo, :] = u[tm - halo:tm, :]

        @pl.when(t == cst_tile)
        def _():
            cst_ref[0] = u[cst_row:cst_row + halo, :]

    o += 3 * C
    zg = mm(o, o + 2 * D)
    sga_ref[...] = jax.nn.sigmoid(zg[:, :D])
    sgc_ref[...] = jax.nn.sigmoid(zg[:, D:])

    o += 2 * D
    zf = mm(o, o + LANES)
    lf_ref[...] = _log_sigmoid(zf + bf_ref[...])[:, :H]


def _inproj(x, g, w, bf, cw, *, A, C, H, scale, seq_len=None, valid_len=None, T=None, halos=None, tm_target=384):
    M, D = x.shape
    halo = cw.shape[0] - 1
    sample = halos is not None
    if sample:
        tm = M
        tiles_per_seq, cst_tile, cst_row = 1, 0, 0
        assert T >= halo
    else:
        tm = _pick_tile(seq_len, tm_target, 16)
        tiles_per_seq = seq_len // tm
        cst_tile, cst_row = divmod(valid_len - halo, tm)
        assert cst_row + halo <= tm
    n_seq = M // (tiles_per_seq * tm)
    Nw = w.shape[1]

    def row_spec(n):
        return pl.BlockSpec((tm, n), lambda i: (i, 0))

    def full_spec(a):
        return pl.BlockSpec(a.shape, lambda i: (0,) * a.ndim)

    in_specs = [row_spec(D), full_spec(g), pl.BlockSpec((D, Nw), lambda i: (0, 0), pipeline_mode=pl.Buffered(1)),
                full_spec(bf), full_spec(cw)]
    args = [x, g, w, bf, cw]
    if sample:
        in_specs += [row_spec(C) for _ in halos]
        args += list(halos)
    if sample:
        kv_dims = (M, A)
        kv_spec = row_spec(A)
    else:
        assert tm % LANES == 0
        kv_dims = (n_seq, A, seq_len)
        kv_spec = pl.BlockSpec((1, A, tm), lambda i: (i // tiles_per_seq, 0, i % tiles_per_seq))
    out_shape = [jax.ShapeDtypeStruct((M, A), BF16), jax.ShapeDtypeStruct(kv_dims, F32),
                 jax.ShapeDtypeStruct(kv_dims, F32), jax.ShapeDtypeStruct(kv_dims, BF16),
                 jax.ShapeDtypeStruct((M, A), BF16), jax.ShapeDtypeStruct((M, H), F32),
                 jax.ShapeDtypeStruct((M, C), BF16), jax.ShapeDtypeStruct((M, D), F32), jax.ShapeDtypeStruct((M, D), F32)]
    out_specs = [row_spec(A), kv_spec, kv_spec, kv_spec, row_spec(A), row_spec(H), row_spec(C), row_spec(D),
                 row_spec(D)]
    scratch = []
    if sample:
        out_shape.append(jax.ShapeDtypeStruct((M, C), F32))
        out_specs.append(row_spec(C))
    else:
        assert 3 * H <= LANES
        out_shape += [jax.ShapeDtypeStruct((n_seq, halo, C), F32), jax.ShapeDtypeStruct((M, LANES), F32)]
        out_specs += [pl.BlockSpec((1, halo, C), lambda i: (i // tiles_per_seq, 0, 0)), row_spec(LANES)]
        scratch.append(pltpu.VMEM((8, C), F32))
    kern = functools.partial(_inproj_kernel, tm=tm, A=A, C=C, D=D, H=H, scale=scale, halo=halo, sample=sample,
                             tiles_per_seq=tiles_per_seq, T=T, cst_tile=cst_tile, cst_row=cst_row)
    return pl.pallas_call(
        kern, grid=(M // tm,), in_specs=in_specs, out_specs=out_specs, out_shape=out_shape,
        scratch_shapes=scratch, compiler_params=_params(("arbitrary",)),
        name="inproj_sample" if sample else "inproj_prompt")(*args)


def _tri_and_ones():
    i = lax.broadcasted_iota(jnp.int32, (LANES, LANES), 0)
    j = lax.broadcasted_iota(jnp.int32, (LANES, LANES), 1)
    return jnp.where(i <= j, 1.0, 0.0).astype(BF16), jnp.ones((LANES, LANES), BF16)


def _cumsum_kernel(x_ref, o_ref, *, n_chunks, R):
    tri, ones = _tri_and_ones()
    x = x_ref[...].reshape(n_chunks * R, LANES)
    hi, mid, lo = _split3(x)
    y = _dot(hi, tri) + _dot(mid, tri) + _dot(lo, tri)
    tot = _dot(hi, ones) + _dot(mid, ones) + _dot(lo, ones)
    carry = jnp.zeros((R, LANES), F32)
    for c in range(n_chunks):
        o_ref[c] = (y[c * R:(c + 1) * R] + carry) * LOG2E
        carry = carry + tot[c * R:(c + 1) * R]


def _cumsum_chunks(x):
    n_chunks, R, _ = x.shape
    return pl.pallas_call(
        functools.partial(_cumsum_kernel, n_chunks=n_chunks, R=R),
        out_shape=jax.ShapeDtypeStruct(x.shape, F32), name="logf_cumsum")(x)


def _attn_kernel(first_ref, q_ref, kt_ref, v_ref, fk_ref, o_ref, qm_scr, m_scr, l_scr, acc_scr, *, tq, tk, n_super,
                 H, Dh):
    qi = pl.program_id(1)
    r = tq // tk
    n_full = qi * r
    n_sub = tk // LANES
    G = LANES // Dh
    lane = lax.broadcasted_iota(jnp.int32, (tq, LANES), 1)

    def group(h):
        return slice((h // G) * LANES, (h // G + 1) * LANES)

    def own_lanes(h):
        return (lane >= (h % G) * Dh) & (lane < (h % G + 1) * Dh)

    for h in range(H):
        qg = q_ref[0, :, group(h)]
        qm_scr[h] = jnp.where(own_lanes(h), qg, jnp.zeros_like(qg))

    def update(chunks, masked):
        starts = [pl.multiple_of(j * tk, tk) for j in chunks]
        for h in range(H):
            parts = []
            for start in starts:
                s = _dot(qm_scr[h], kt_ref[0, group(h), pl.ds(start, tk)])
                s = s - fk_ref[0, h:h + 1, pl.ds(start, tk)]
                if masked:
                    qpos = qi * tq + lax.broadcasted_iota(jnp.int32, (tq, tk), 0)
                    kpos = start + lax.broadcasted_iota(jnp.int32, (tq, tk), 1)
                    s = jnp.where(kpos <= qpos, s, NEG_BIG)
                parts += [s[:, c * LANES:(c + 1) * LANES] for c in range(n_sub)]
            m_old = m_scr[h]
            row_max = jnp.max(functools.reduce(jnp.maximum, parts), axis=1, keepdims=True)
            m_new = jnp.maximum(m_old, jnp.broadcast_to(row_max, (tq, LANES)))
            alpha = jnp.exp2(m_old - m_new)
            probs = [jnp.exp2(part - m_new) for part in parts]
            l_scr[h] = alpha * l_scr[h] + functools.reduce(jnp.add, probs)
            pv = [_dot(jnp.concatenate(probs[i * n_sub:(i + 1) * n_sub], axis=1).astype(BF16),
                       v_ref[0, pl.ds(start, tk), group(h)]) for i, start in enumerate(starts)]
            acc_scr[h] = alpha * acc_scr[h] + functools.reduce(jnp.add, pv)
            m_scr[h] = m_new

    m_scr[...] = jnp.full_like(m_scr, NEG_BIG)
    l_scr[...] = jnp.zeros_like(l_scr)
    acc_scr[...] = jnp.zeros_like(acc_scr)

    first = first_ref[pl.program_id(0) * pl.num_programs(1) + qi]
    count = n_full - first

    def body(jj, carry):
        update([first + n_super * jj + i for i in range(n_super)], False)
        return carry

    lax.fori_loop(0, count // n_super, body, 0)
    done = first + (count // n_super) * n_super
    size = n_super // 2
    while size >= 1:
        take = (count & size) != 0

        @pl.when(take)
        def _(done=done, size=size):
            update([done + i for i in range(size)], False)

        done = done + jnp.where(take, size, 0)
        size //= 2
    for d in range(r):
        update([n_full + d], True)

    for g in range(H // G):
        out = jnp.zeros((tq, LANES), F32)
        for h in range(g * G, (g + 1) * G):
            l = jnp.sum(l_scr[h], axis=1, keepdims=True)
            out = jnp.where(own_lanes(h), acc_scr[h] / l, out)
        o_ref[0, :, group(g * G)] = out.astype(BF16)


SKIP_LOG2 = 160.0
NORM_SLACK = 1.05


def _first_live_chunk(stats, fk, *, H, tq, tk):
    B, Lp, _ = stats.shape
    n_q, n_k = Lp // tq, Lp // tk
    fk_rows = fk.transpose(0, 2, 1)
    q_norm = jnp.sqrt(stats[..., 0:H]).reshape(B, n_q, tq, H).max(axis=2)
    k_norm = jnp.sqrt(stats[..., H:2 * H]).reshape(B, n_k, tk, H).max(axis=2)
    own = (stats[..., 2 * H:3 * H] - fk_rows).reshape(B, n_q, tq, H).min(axis=2)
    fk_min = fk_rows.reshape(B, n_k, tk, H).min(axis=2)
    bound = (NORM_SLACK * q_norm[:, :, None] * k_norm[:, None] - fk_min[:, None] - own[:, :, None])
    live = jnp.any(bound >= -SKIP_LOG2, axis=-1)
    live = live | (jnp.arange(n_k)[None, None, :] * tk + tk > jnp.arange(n_q)[None, :, None] * tq)
    return jnp.argmax(live, axis=-1).astype(jnp.int32).reshape(B * n_q)


def _attn_prompt(q, kt, v, fk, stats, *, H, Dh, tq=256, tk=256, n_super=4):
    B, Lp, A = q.shape
    assert LANES % Dh == 0 and H % (LANES // Dh) == 0 and n_super & (n_super - 1) == 0 and tq == tk
    first = _first_live_chunk(stats, fk, H=H, tq=tq, tk=tk)
    stat = pltpu.VMEM((H, tq, LANES), F32)
    grid_spec = pltpu.PrefetchScalarGridSpec(
        num_scalar_prefetch=1, grid=(B, Lp // tq),
        in_specs=[pl.BlockSpec((1, tq, A), lambda b, i, first: (b, i, 0)),
                  pl.BlockSpec((1, A, Lp), lambda b, i, first: (b, 0, 0), pipeline_mode=pl.Buffered(1)),
                  pl.BlockSpec((1, Lp, A), lambda b, i, first: (b, 0, 0), pipeline_mode=pl.Buffered(1)),
                  pl.BlockSpec((1, H, Lp), lambda b, i, first: (b, 0, 0))],
        out_specs=pl.BlockSpec((1, tq, A), lambda b, i, first: (b, i, 0)),
        scratch_shapes=[pltpu.VMEM((H, tq, LANES), BF16), stat, stat, stat])
    return pl.pallas_call(
        functools.partial(_attn_kernel, tq=tq, tk=tk, n_super=n_super, H=H, Dh=Dh), grid_spec=grid_spec,
        out_shape=jax.ShapeDtypeStruct((B, Lp, A), BF16),
        compiler_params=_params(("arbitrary", "arbitrary")), name="attn_prompt")(first, q, kt, v, fk)


def _attn_sample_kernel(pt_ref, qbd_ref, knew_ref, vnew_ref, lfnew_ref, *refs, PP, T, H, P):
    k_refs, v_refs, lf_refs = refs[:PP], refs[PP:2 * PP], refs[2 * PP:3 * PP]
    o_ref, m_scr, l_scr, acc_scr, carry_scr = refs[3 * PP:]
    j = pl.program_id(1)
    R = T * H
    contract_last = (((1,), (1,)), ((), ()))

    @pl.when(j == 0)
    def _():
        m_scr[...] = jnp.full_like(m_scr, NEG_BIG)
        l_scr[...] = jnp.zeros_like(l_scr)
        acc_scr[...] = jnp.zeros_like(acc_scr)
        carry_scr[...] = jnp.zeros_like(carry_scr)

    tri_ones = jnp.concatenate(_tri_and_ones(), axis=1)
    qbd = qbd_ref[0]

    def pages(state, lfs, kts, vts, mask):
        m, l, acc, carry = state
        n = len(lfs)
        pad = (-H) % 16
        Hp = H + pad
        zeros = [jnp.zeros((pad, P), F32)] if pad else []
        terms = _split3(jnp.concatenate([piece for lf in lfs for piece in [lf] + zeros], axis=0))
        y = _dot(jnp.concatenate(terms, axis=0), tri_ones)
        y = y[0:n * Hp] + y[n * Hp:2 * n * Hp] + y[2 * n * Hp:3 * n * Hp]
        ss = []
        for i in range(n):
            fk = carry + y[i * Hp:i * Hp + H, :P]
            carry = carry + y[i * Hp:i * Hp + H, P:]
            s = _dot(qbd, kts[i]) - jnp.concatenate([fk * LOG2E] * T, axis=0)
            ss.append(s if mask is None else jnp.where(mask, s, NEG_BIG))
        m_new = jnp.maximum(m, jnp.max(functools.reduce(jnp.maximum, ss), axis=1, keepdims=True))
        alpha = jnp.exp2(m - m_new)
        ps = [jnp.exp2(s - m_new) for s in ss]
        l = alpha * l + jnp.sum(functools.reduce(jnp.add, ps), axis=1, keepdims=True)
        pv = [lax.dot_general(p.astype(BF16), vt, contract_last, preferred_element_type=F32)
              for p, vt in zip(ps, vts)]
        acc = alpha * acc + functools.reduce(jnp.add, pv)
        return m_new, l, acc, carry

    state = (m_scr[...], l_scr[...], acc_scr[...], carry_scr[...])
    state = pages(state, [r[0, 0] for r in lf_refs], [r[0, 0].astype(BF16) for r in k_refs],
                  [r[0, 0].astype(BF16) for r in v_refs], None)
    m_scr[...], l_scr[...], acc_scr[...], carry_scr[...] = state

    @pl.when(j == pl.num_programs(1) - 1)
    def _():
        t_row = lax.broadcasted_iota(jnp.int32, (R, P), 0) // H
        key = lax.broadcasted_iota(jnp.int32, (R, P), 1)
        _, l, acc, _ = pages(state, [lfnew_ref[0]], [knew_ref[0]], [vnew_ref[0]], key <= t_row)
        o_ref[0] = acc / l


def _attn_sample(page_table, qbd, k_new, v_new, lf_new, cache_k, cache_v, cache_lf, *, layer, T, H, pages_per_step=16):
    n_seq, n_pages = page_table.shape
    _, _, A, P = cache_k.shape
    R = T * H
    PP = _pick_tile(n_pages, pages_per_step, 1)

    def seq_spec(shape):
        return pl.BlockSpec((1,) + shape, lambda s, j, pt: (s, 0, 0))

    def page_spec(shape, p):
        return pl.BlockSpec((1, 1) + shape, lambda s, j, pt: (layer, pt[s, j * PP + p], 0, 0))

    in_specs = ([seq_spec((R, A)), seq_spec((A, P)), seq_spec((A, P)), seq_spec((H, P))]
                + [page_spec((A, P), p) for p in range(PP)] + [page_spec((A, P), p) for p in range(PP)]
                + [page_spec((H, P), p) for p in range(PP)])
    grid_spec = pltpu.PrefetchScalarGridSpec(
        num_scalar_prefetch=1, grid=(n_seq, n_pages // PP), in_specs=in_specs,
        out_specs=pl.BlockSpec((1, R, A), lambda s, j, pt: (s, 0, 0)),
        scratch_shapes=[pltpu.VMEM((R, 1), F32), pltpu.VMEM((R, 1), F32), pltpu.VMEM((R, A), F32),
                        pltpu.VMEM((H, P), F32)])
    return pl.pallas_call(
        functools.partial(_attn_sample_kernel, PP=PP, T=T, H=H, P=P), grid_spec=grid_spec,
        out_shape=jax.ShapeDtypeStruct((n_seq, R, A), F32),
        compiler_params=_params(("arbitrary", "arbitrary")), name="attn_sample")(
            page_table, qbd, k_new, v_new, lf_new, *([cache_k] * PP), *([cache_v] * PP), *([cache_lf] * PP))


def _merge_kernel(o_ref, cy_ref, sga_ref, sgc_ref, x_ref, wa_ref, wc_ref, wo_ref, out_ref):
    ya = _dot(o_ref[...], wa_ref[...])
    yc = _dot(cy_ref[...], wc_ref[...])
    mix = (sga_ref[...] * ya + sgc_ref[...] * yc).astype(BF16)
    out_ref[...] = x_ref[...] + _dot(mix, wo_ref[...])


def _merge(o, cy, sga, sgc, x, wa, wc, wo, *, tm_target=512):
    M, D = x.shape
    tm = _pick_tile(M, tm_target, 16)

    def row_spec(n):
        return pl.BlockSpec((tm, n), lambda i: (i, 0))

    def w_spec(a):
        return pl.BlockSpec(a.shape, lambda i: (0, 0))

    return pl.pallas_call(
        _merge_kernel, grid=(M // tm,),
        in_specs=[row_spec(o.shape[1]), row_spec(cy.shape[1]), row_spec(D), row_spec(D), row_spec(D),
                  w_spec(wa), w_spec(wc), w_spec(wo)],
        out_specs=row_spec(D), out_shape=jax.ShapeDtypeStruct((M, D), F32),
        compiler_params=_params(("arbitrary",)), name="merge")(o, cy, sga, sgc, x, wa, wc, wo)


def _swiglu_step(xb, wg, wu, wd, row_scale=None):
    gate = _dot(xb, wg)
    up = _dot(xb, wu)
    h = jax.nn.silu(gate) * up
    if row_scale is not None:
        h = h * row_scale
    return _dot(h.astype(BF16), wd)


def _finish(x, acc, gfin_ref, out_ref):
    y = x + acc
    if gfin_ref is not None:
        y = _rms(y, gfin_ref[...])
    out_ref[...] = y


def _ffn_kernel(*refs, final):
    if final:
        x_ref, g_ref, wg_ref, wu_ref, wd_ref, gfin_ref, out_ref, xn_scr, acc_scr = refs
    else:
        x_ref, g_ref, wg_ref, wu_ref, wd_ref, out_ref, xn_scr, acc_scr = refs
        gfin_ref = None
    f = pl.program_id(1)

    @pl.when(f == 0)
    def _():
        xn_scr[...] = _rms(x_ref[...], g_ref[...]).astype(BF16)
        acc_scr[...] = jnp.zeros_like(acc_scr)

    acc_scr[...] += _swiglu_step(xn_scr[...], wg_ref[...], wu_ref[...], wd_ref[...])

    @pl.when(f == pl.num_programs(1) - 1)
    def _():
        _finish(x_ref[...], acc_scr[...], gfin_ref, out_ref)


def _ffn(x, g, wg, wu, wd, gfin=None, *, tm_target=1024, tf_target=512):
    M, D = x.shape
    F = wg.shape[1]
    tm = _pick_tile(M, tm_target, 16)
    tf = _pick_tile(F, tf_target, LANES)
    final = gfin is not None
    in_specs = [pl.BlockSpec((tm, D), lambda i, f: (i, 0)), pl.BlockSpec((1, D), lambda i, f: (0, 0)),
                pl.BlockSpec((D, tf), lambda i, f: (0, f)), pl.BlockSpec((D, tf), lambda i, f: (0, f)),
                pl.BlockSpec((tf, D), lambda i, f: (f, 0))]
    args = [x, g, wg, wu, wd]
    if final:
        in_specs.append(pl.BlockSpec((1, D), lambda i, f: (0, 0)))
        args.append(gfin)
    return pl.pallas_call(
        functools.partial(_ffn_kernel, final=final), grid=(M // tm, F // tf), in_specs=in_specs,
        out_specs=pl.BlockSpec((tm, D), lambda i, f: (i, 0)), out_shape=jax.ShapeDtypeStruct((M, D), F32),
        scratch_shapes=[pltpu.VMEM((tm, D), BF16), pltpu.VMEM((tm, D), F32)],
        compiler_params=_params(("arbitrary", "arbitrary")), name="ffn_dense")(*args)


def _route(xn, wr_hi, wr_lo, n_experts):
    hi = xn.astype(BF16)
    lo = (xn - hi.astype(F32)).astype(BF16)
    logits = _dot(hi, wr_hi) + _dot(lo, wr_hi) + _dot(hi, wr_lo)
    lane = lax.broadcasted_iota(jnp.int32, logits.shape, 1).astype(F32)
    lg = jnp.where(lane < n_experts, logits, NEG_BIG)
    v1 = jnp.max(lg, axis=1, keepdims=True)
    i1 = jnp.min(jnp.where(lg == v1, lane, float(LANES)), axis=1, keepdims=True)
    lg2 = jnp.where(lane == i1, NEG_BIG, lg)
    v2 = jnp.max(lg2, axis=1, keepdims=True)
    i2 = jnp.min(jnp.where(lg2 == v2, lane, float(LANES)), axis=1, keepdims=True)
    e2 = jnp.exp(v2 - v1)
    den = 1.0 + e2
    return jnp.where(lane == i1, 1.0 / den, 0.0) + jnp.where(lane == i2, e2 / den, 0.0)


def _moe_dense_kernel(*refs, final, n_experts):
    if final:
        x_ref, g_ref, wrh_ref, wrl_ref, wg_ref, wu_ref, wd_ref, gfin_ref, out_ref, xn_scr, acc_scr, comb_scr = refs
    else:
        x_ref, g_ref, wrh_ref, wrl_ref, wg_ref, wu_ref, wd_ref, out_ref, xn_scr, acc_scr, comb_scr = refs
        gfin_ref = None
    e = pl.program_id(1)
    f = pl.program_id(2)

    @pl.when((e == 0) & (f == 0))
    def _():
        xn = _rms(x_ref[...], g_ref[...])
        xn_scr[...] = xn.astype(BF16)
        comb_scr[...] = _route(xn, wrh_ref[...], wrl_ref[...], n_experts)
        acc_scr[...] = jnp.zeros_like(acc_scr)

    comb = comb_scr[...]
    lane = lax.broadcasted_iota(jnp.int32, comb.shape, 1)
    ce = jnp.sum(jnp.where(lane == e, comb, 0.0), axis=1, keepdims=True)
    acc_scr[...] += _swiglu_step(xn_scr[...], wg_ref[0], wu_ref[0], wd_ref[0], row_scale=ce)

    @pl.when((e == pl.num_programs(1) - 1) & (f == pl.num_programs(2) - 1))
    def _():
        _finish(x_ref[...], acc_scr[...], gfin_ref, out_ref)


def _moe_dense(x, g, wr_hi, wr_lo, wg, wu, wd, gfin=None, *, tm_target=1024, tf_target=512):
    M, D = x.shape
    E, _, F = wg.shape
    tm = _pick_tile(M, tm_target, 16)
    tf = _pick_tile(F, tf_target, LANES)
    final = gfin is not None
    vec = pl.BlockSpec((1, D), lambda i, e, f: (0, 0))
    wr_spec = pl.BlockSpec((D, LANES), lambda i, e, f: (0, 0))
    in_specs = [pl.BlockSpec((tm, D), lambda i, e, f: (i, 0)), vec, wr_spec, wr_spec,
                pl.BlockSpec((1, D, tf), lambda i, e, f: (e, 0, f)), pl.BlockSpec((1, D, tf), lambda i, e, f: (e, 0, f)),
                pl.BlockSpec((1, tf, D), lambda i, e, f: (e, f, 0))]
    args = [x, g, wr_hi, wr_lo, wg, wu, wd]
    if final:
        in_specs.append(vec)
        args.append(gfin)
    return pl.pallas_call(
        functools.partial(_moe_dense_kernel, final=final, n_experts=E), grid=(M // tm, E, F // tf),
        in_specs=in_specs, out_specs=pl.BlockSpec((tm, D), lambda i, e, f: (i, 0)),
        out_shape=jax.ShapeDtypeStruct((M, D), F32),
        scratch_shapes=[pltpu.VMEM((tm, D), BF16), pltpu.VMEM((tm, D), F32), pltpu.VMEM((tm, LANES), F32)],
        compiler_params=_params(("arbitrary", "arbitrary", "arbitrary")), name="moe_dense")(*args)


SEG_ROWS = 16


def _selected(comb):
    return jnp.where(comb > 0.0, 1.0, 0.0)


def _positions(sel_bf16, tm):
    r = lax.broadcasted_iota(jnp.int32, (tm, tm), 0)
    c = lax.broadcasted_iota(jnp.int32, (tm, tm), 1)
    return _dot(jnp.where(c < r, 1.0, 0.0).astype(BF16), sel_bf16)


def _moe_route_kernel(x_ref, g_ref, wrh_ref, wrl_ref, xn_ref, comb_ref, cnt_ref, *, tm, n_experts, seq_len, valid_len):
    xn = _rms(x_ref[...], g_ref[...])
    xn_ref[...] = xn.astype(BF16)
    comb = _route(xn, wrh_ref[...], wrl_ref[...], n_experts)
    row = pl.program_id(0) * tm + lax.broadcasted_iota(jnp.int32, (tm, 1), 0)
    comb = jnp.where(row % seq_len < valid_len, comb, 0.0)
    comb_ref[...] = comb
    cnt_ref[0] = jnp.sum(_selected(comb), axis=0, keepdims=True)


def _segment_copies(meta, tile, n_experts, make_copy, wait):
    cnt_ref, loff_ref, goff_ref = meta
    for e in range(n_experts):
        idx = tile * n_experts + e
        local0, global0 = loff_ref[idx], goff_ref[idx]

        def body(k, carry, local0=local0, global0=global0):
            cp = make_copy(pl.multiple_of(local0 + k * SEG_ROWS, SEG_ROWS),
                           pl.multiple_of(global0 + k * SEG_ROWS, SEG_ROWS))
            if wait:
                cp.wait()
            else:
                cp.start()
            return carry

        lax.fori_loop(0, cnt_ref[idx] // SEG_ROWS, body, 0)


def _moe_dispatch_kernel(cnt_ref, loff_ref, goff_ref, xn_ref, comb_ref, loffv_ref, xs_in_ref, xs_ref,
                         stage_scr, sem, *, tm, S, n_experts):
    del xs_in_ref
    tile = pl.program_id(0)
    sel = _selected(comb_ref[...])
    slot = loffv_ref[0] + _positions(sel.astype(BF16), tm)
    slot_t = jnp.where(sel > 0.0, slot, -1.0).T
    rows = lax.broadcasted_iota(jnp.int32, (S, tm), 0).astype(F32)
    onehot = jnp.zeros((S, tm), F32)
    for e in range(n_experts):
        onehot = onehot + jnp.where(slot_t[e:e + 1, :] == rows, 1.0, 0.0)
    stage_scr[...] = _dot(onehot.astype(BF16), xn_ref[...]).astype(BF16)

    def make_copy(local, glob):
        return pltpu.make_async_copy(stage_scr.at[pl.ds(local, SEG_ROWS)], xs_ref.at[pl.ds(glob, SEG_ROWS)], sem)

    meta = (cnt_ref, loff_ref, goff_ref)
    _segment_copies(meta, tile, n_experts, make_copy, wait=False)
    _segment_copies(meta, tile, n_experts, make_copy, wait=True)


def _moe_group_kernel(blk_ref, nact_ref, xs_ref, wg_ref, wu_ref, wd_ref, ys_ref, acc_scr):
    del blk_ref
    i = pl.program_id(0)
    f = pl.program_id(1)
    active = i < nact_ref[0]

    @pl.when(f == 0)
    def _():
        acc_scr[...] = jnp.zeros_like(acc_scr)

    @pl.when(active)
    def _():
        acc_scr[...] += _swiglu_step(xs_ref[...], wg_ref[0], wu_ref[0], wd_ref[0])

    @pl.when(f == pl.num_programs(1) - 1)
    def _():
        ys_ref[...] = acc_scr[...].astype(BF16)


def _moe_combine_kernel(cnt_ref, loff_ref, goff_ref, x_ref, comb_ref, loffv_ref, ys_ref, *refs,
                        tm, S, n_experts, final):
    if final:
        gfin_ref, out_ref, stage_scr, sem = refs
    else:
        out_ref, stage_scr, sem = refs
        gfin_ref = None
    tile = pl.program_id(0)

    @pl.when(tile == 0)
    def _():
        stage_scr[...] = jnp.zeros_like(stage_scr)

    def make_copy(local, glob):
        return pltpu.make_async_copy(ys_ref.at[pl.ds(glob, SEG_ROWS)], stage_scr.at[pl.ds(local, SEG_ROWS)], sem)

    meta = (cnt_ref, loff_ref, goff_ref)
    _segment_copies(meta, tile, n_experts, make_copy, wait=False)

    comb = comb_ref[...]
    lane = lax.broadcasted_iota(jnp.int32, comb.shape, 1).astype(F32)
    slot = loffv_ref[0] + _positions(_selected(comb).astype(BF16), tm)
    cols = lax.broadcasted_iota(jnp.int32, (tm, S), 1).astype(F32)
    g1 = jnp.max(comb, axis=1, keepdims=True)
    i1 = jnp.min(jnp.where(comb == g1, lane, float(LANES)), axis=1, keepdims=True)
    rest = jnp.where(lane == i1, -1.0, comb)
    g2 = jnp.max(rest, axis=1, keepdims=True)
    i2 = jnp.min(jnp.where(rest == g2, lane, float(LANES)), axis=1, keepdims=True)
    picks = []
    for gate, idx in ((g1, i1), (g2, i2)):
        s_k = jnp.sum(jnp.where(lane == idx, slot, 0.0), axis=1, keepdims=True)
        picks.append((gate, jnp.where(cols == s_k, 1.0, 0.0).astype(BF16)))

    _segment_copies(meta, tile, n_experts, make_copy, wait=True)
    stage = stage_scr[...]
    acc = picks[0][0] * _dot(picks[0][1], stage) + picks[1][0] * _dot(picks[1][1], stage)
    _finish(x_ref[...], acc, gfin_ref, out_ref)


def _moe_routed(x, g, wr_hi, wr_lo, wg, wu, wd, gfin=None, *, seq_len, valid_len, tm_target=512, block_rows=1024,
                tf_target=512):
    M, D = x.shape
    E, _, F = wg.shape
    tm = _pick_tile(M, tm_target, LANES)
    n_tiles = M // tm
    S = TOP_K * tm + SEG_ROWS * E
    BR = block_rows
    tf = _pick_tile(F, tf_target, LANES)
    n_rows = -(-(TOP_K * M + n_tiles * E * (SEG_ROWS - 1) + E * (BR - 1)) // BR) * BR
    n_blocks = n_rows // BR
    final = gfin is not None

    def tile_spec(n):
        return pl.BlockSpec((tm, n), lambda i, *_: (i, 0))

    def const_spec(a):
        return pl.BlockSpec(a.shape, lambda i, *_: (0,) * a.ndim)

    xn, comb, cnt = pl.pallas_call(
        functools.partial(_moe_route_kernel, tm=tm, n_experts=E, seq_len=seq_len, valid_len=valid_len),
        grid=(n_tiles,),
        in_specs=[tile_spec(D), const_spec(g), const_spec(wr_hi), const_spec(wr_lo)],
        out_specs=[tile_spec(D), tile_spec(LANES), pl.BlockSpec((1, 1, LANES), lambda i: (i, 0, 0))],
        out_shape=[jax.ShapeDtypeStruct((M, D), BF16), jax.ShapeDtypeStruct((M, LANES), F32),
                   jax.ShapeDtypeStruct((n_tiles, 1, LANES), F32)],
        compiler_params=_params(("arbitrary",)), name="moe_route")(x, g, wr_hi, wr_lo)

    cnt = cnt[:, 0, :E].astype(jnp.int32)
    cnt_pad = -(-cnt // SEG_ROWS) * SEG_ROWS
    loff = jnp.cumsum(cnt_pad, axis=1) - cnt_pad
    region = -(-jnp.sum(cnt_pad, axis=0) // BR) * BR
    ends = jnp.cumsum(region)
    goff = (ends - region)[None, :] + jnp.cumsum(cnt_pad, axis=0) - cnt_pad
    blk_expert = jnp.minimum(jnp.searchsorted(ends // BR, jnp.arange(n_blocks), side="right"), E - 1).astype(jnp.int32)
    n_active = (ends[-1:] // BR).astype(jnp.int32)
    meta = (cnt_pad.reshape(-1), loff.reshape(-1), goff.reshape(-1))
    loff_vec = jnp.pad(loff.astype(F32), ((0, 0), (0, LANES - E))).reshape(n_tiles, 1, LANES)
    loffv_spec = pl.BlockSpec((1, 1, LANES), lambda i, *_: (i, 0, 0))
    any_spec = pl.BlockSpec(memory_space=pl.ANY)

    xs = pl.pallas_call(
        functools.partial(_moe_dispatch_kernel, tm=tm, S=S, n_experts=E),
        grid_spec=pltpu.PrefetchScalarGridSpec(
            num_scalar_prefetch=3, grid=(n_tiles,),
            in_specs=[tile_spec(D), tile_spec(LANES), loffv_spec, any_spec], out_specs=any_spec,
            scratch_shapes=[pltpu.VMEM((S, D), BF16), pltpu.SemaphoreType.DMA(())]),
        out_shape=jax.ShapeDtypeStruct((n_rows, D), BF16), input_output_aliases={6: 0},
        compiler_params=_params(("arbitrary",)), name="moe_dispatch")(
            *meta, xn, comb, loff_vec, jnp.zeros((n_rows, D), BF16))

    ys = pl.pallas_call(
        _moe_group_kernel,
        grid_spec=pltpu.PrefetchScalarGridSpec(
            num_scalar_prefetch=2, grid=(n_blocks, F // tf),
            in_specs=[pl.BlockSpec((BR, D), lambda i, f, blk, nact: (i, 0)),
                      pl.BlockSpec((1, D, tf), lambda i, f, blk, nact: (blk[i], 0, f)),
                      pl.BlockSpec((1, D, tf), lambda i, f, blk, nact: (blk[i], 0, f)),
                      pl.BlockSpec((1, tf, D), lambda i, f, blk, nact: (blk[i], f, 0))],
            out_specs=pl.BlockSpec((BR, D), lambda i, f, blk, nact: (i, 0)),
            scratch_shapes=[pltpu.VMEM((BR, D), F32)]),
        out_shape=jax.ShapeDtypeStruct((n_rows, D), BF16),
        compiler_params=_params(("arbitrary", "arbitrary")), name="moe_group")(blk_expert, n_active, xs, wg, wu, wd)

    in_specs = [tile_spec(D), tile_spec(LANES), loffv_spec, any_spec]
    args = [x, comb, loff_vec, ys]
    if final:
        in_specs.append(const_spec(gfin))
        args.append(gfin)
    return pl.pallas_call(
        functools.partial(_moe_combine_kernel, tm=tm, S=S, n_experts=E, final=final),
        grid_spec=pltpu.PrefetchScalarGridSpec(
            num_scalar_prefetch=3, grid=(n_tiles,), in_specs=in_specs, out_specs=tile_spec(D),
            scratch_shapes=[pltpu.VMEM((S, D), BF16), pltpu.SemaphoreType.DMA(())]),
        out_shape=jax.ShapeDtypeStruct((M, D), F32),
        compiler_params=_params(("arbitrary",)), name="moe_combine")(*meta, *args)


def kernel(x_prompt, x_sample, cache_k, cache_v, cache_logf, state_conv, page_table, meta_tokens, norm_mix, w_in,
           b_forget, conv_w, w_out_attn, w_out_conv, w_o, norm_ffn, ffn_w_gate, ffn_w_up, ffn_w_down, moe_router,
           moe_w_gate, moe_w_up, moe_w_down, norm_final):
    B, S, D = x_prompt.shape
    NB, T, _ = x_sample.shape
    depth, n_pool, P, H, Dh = cache_k.shape
    n_meta = meta_tokens.shape[0]
    A = H * Dh
    C = conv_w.shape[2]
    halo = conv_w.shape[1] - 1
    E = moe_router.shape[2]
    L = S + n_meta
    Lp = -(-L // 256) * 256
    n_chunks = Lp // LANES
    scale = float(Dh) ** -0.5 * LOG2E

    o_f = 3 * A
    o_c = o_f + H

    def prep_w_in(w):
        wf = jnp.pad(w[:, o_f:o_c], ((0, 0), (0, LANES - H)))
        return jnp.concatenate([w[:, :o_f], w[:, o_c:], wf], axis=1).astype(BF16)

    def row(v):
        return v.reshape(1, -1)

    meta = jnp.broadcast_to(meta_tokens[None].astype(x_prompt.dtype), (B, n_meta, D))
    x_p = jnp.concatenate([meta, x_prompt, jnp.zeros((B, Lp - L, D), x_prompt.dtype)], axis=1).reshape(B * Lp, D)
    x_s = x_sample.reshape(NB * T, D)
    eye_h = jnp.eye(H, dtype=F32)

    outs = {n: [] for n in ("kp", "vp", "lfp", "cp", "ks", "vs", "lfs", "cs")}
    cache_kt = cache_k.transpose(0, 1, 3, 4, 2).reshape(depth, n_pool, A, P)
    cache_vt = cache_v.transpose(0, 1, 3, 4, 2).reshape(depth, n_pool, A, P)
    cache_lft = cache_logf.transpose(0, 1, 3, 2)
    for l in range(depth):
        last = l == depth - 1
        w_l = prep_w_in(w_in[l])
        bf_l = jnp.pad(b_forget[l], (0, LANES - H)).reshape(1, LANES)
        g_l = row(norm_mix[l])
        wa, wc, wo = w_out_attn[l].astype(BF16), w_out_conv[l].astype(BF16), w_o[l].astype(BF16)

        q, kt, vt, ktb, vb, lf, cy, sga, sgc, cst, stats = _inproj(
            x_p, g_l, w_l, bf_l, conv_w[l], A=A, C=C, H=H, scale=scale, seq_len=Lp, valid_len=L)
        outs["kp"].append(kt[:, :, :L].reshape(B, H, Dh, L))
        outs["vp"].append(vt[:, :, :L].reshape(B, H, Dh, L))
        lf_chunks = lf.reshape(B, n_chunks, LANES, H).transpose(1, 0, 3, 2).reshape(n_chunks, B * H, LANES)
        fk = _cumsum_chunks(lf_chunks).reshape(n_chunks, B, H, LANES).transpose(1, 2, 0, 3).reshape(B, H, Lp)
        o_a = _attn_prompt(q.reshape(B, Lp, A), ktb, vb.reshape(B, Lp, A), fk, stats.reshape(B, Lp, LANES), H=H,
                           Dh=Dh)
        x_p = _merge(o_a.reshape(B * Lp, A), cy, sga, sgc, x_p, wa, wc, wo)
        outs["lfp"].append(lf.reshape(B, Lp, H)[:, :L])
        outs["cp"].append(cst)

        st = state_conv[l]
        halos = [jnp.pad(st[:, halo - s:], ((0, 0), (0, T - s), (0, 0))).reshape(NB * T, C) for s in range(1, halo + 1)]
        q, k, v, kb, vb, lf, cy, sga, sgc, u = _inproj(
            x_s, g_l, w_l, bf_l, conv_w[l], A=A, C=C, H=H, scale=scale, T=T, halos=halos)
        qbd = (q.reshape(NB, T, H, 1, Dh) * eye_h.astype(BF16)[None, None, :, :, None]).reshape(NB, T * H, A)
        pad_keys = ((0, 0), (0, 0), (0, P - T))
        o_bd = _attn_sample(
            page_table, qbd, jnp.pad(kb.reshape(NB, T, A).transpose(0, 2, 1), pad_keys),
            jnp.pad(vb.reshape(NB, T, A).transpose(0, 2, 1), pad_keys),
            jnp.pad(lf.reshape(NB, T, H).transpose(0, 2, 1), pad_keys),
            cache_kt, cache_vt, cache_lft, layer=l, T=T, H=H)
        o_a = jnp.sum(o_bd.reshape(NB, T, H, H, Dh) * eye_h[None, None, :, :, None], axis=2).reshape(NB * T, A)
        x_s = _merge(o_a.astype(BF16), cy, sga, sgc, x_s, wa, wc, wo)
        outs["ks"].append(k.reshape(NB, T, H, Dh))
        outs["vs"].append(v.reshape(NB, T, H, Dh))
        outs["lfs"].append(lf.reshape(NB, T, H))
        outs["cs"].append(u.reshape(NB, T, C)[:, T - halo:])

        gf = row(norm_ffn[l])
        gfin = row(norm_final) if last else None
        j = l // 2
        if l % 2 == 0:
            wg, wu, wd = ffn_w_gate[j].astype(BF16), ffn_w_up[j].astype(BF16), ffn_w_down[j].astype(BF16)
            x_p = _ffn(x_p, gf, wg, wu, wd, gfin)
            x_s = _ffn(x_s, gf, wg, wu, wd, gfin)
        else:
            wr = jnp.pad(moe_router[j], ((0, 0), (0, LANES - E)))
            wr_hi = wr.astype(BF16)
            wr_lo = (wr - wr_hi.astype(F32)).astype(BF16)
            wg, wu, wd = moe_w_gate[j].astype(BF16), moe_w_up[j].astype(BF16), moe_w_down[j].astype(BF16)
            x_p = _moe_routed(x_p, gf, wr_hi, wr_lo, wg, wu, wd, gfin, seq_len=Lp, valid_len=L)
            x_s = _moe_dense(x_s, gf, wr_hi, wr_lo, wg, wu, wd, gfin)

    y_prompt = x_p.reshape(B, Lp, D)[:, n_meta:L]
    y_sample = x_s.reshape(NB, T, D)
    k_prompt = jnp.stack(outs["kp"]).transpose(0, 1, 4, 2, 3)
    v_prompt = jnp.stack(outs["vp"]).transpose(0, 1, 4, 2, 3)
    return (y_prompt, y_sample, k_prompt, v_prompt, jnp.stack(outs["lfp"]),
            jnp.stack(outs["cp"]), jnp.stack(outs["ks"]), jnp.stack(outs["vs"]), jnp.stack(outs["lfs"]),
            jnp.stack(outs["cs"]))
```

```python
import functools

import jax
import jax.numpy as jnp
from jax import lax
from jax.experimental import pallas as pl
from jax.experimental.pallas import tpu as pltpu

RMS_EPS = 1e-6
TOP_K = 2
NEG_BIG = -1e30
LOG2E = 1.4426950408889634
LANES = 128
V7X_VMEM_LIMIT_BYTES = 56 * 1024 * 1024

F32 = jnp.float32
BF16 = jnp.bfloat16


def _pick_tile(n, target, mult=8):
    best = None
    for t in range(mult, min(n, target) + 1, mult):
        if n % t == 0:
            best = t
    assert best is not None, (n, target, mult)
    return best


def _params(sem, vmem=V7X_VMEM_LIMIT_BYTES):
    return pltpu.CompilerParams(dimension_semantics=sem, vmem_limit_bytes=vmem)


def _rms(x, g):
    ms = jnp.mean(x * x, axis=-1, keepdims=True)
    return (x * lax.rsqrt(ms + RMS_EPS)) * g


def _log_sigmoid(x):
    return jnp.minimum(x, 0.0) - jnp.log1p(jnp.exp(-jnp.abs(x)))


def _split3(x):
    hi = x.astype(BF16)
    r1 = x - hi.astype(F32)
    mid = r1.astype(BF16)
    lo = (r1 - mid.astype(F32)).astype(BF16)
    return hi, mid, lo


def _dot(a, b):
    return jnp.dot(a, b, preferred_element_type=F32)


def _dot3(x, w_bf16):
    hi, mid, lo = _split3(x)
    return _dot(hi, w_bf16) + _dot(mid, w_bf16) + _dot(lo, w_bf16)


def _inproj_kernel(*refs, tm, A, C, D, H, scale, halo, sample, tiles_per_seq, T, cst_tile, cst_row):
    if sample:
        (x_ref, g_ref, w_ref, bf_ref, cw_ref, *halo_refs) = refs[: 5 + halo]
        (q_ref, k_ref, v_ref, kb_ref, vb_ref, lf_ref, cy_ref, sga_ref, sgc_ref, u_ref) = refs[5 + halo:]
    else:
        (x_ref, g_ref, w_ref, bf_ref, cw_ref,
         q_ref, k_ref, v_ref, kb_ref, vb_ref, lf_ref, cy_ref, sga_ref, sgc_ref, cst_ref, stats_ref,
         carry_scr) = refs

    xb = _rms(x_ref[...], g_ref[...]).astype(BF16)

    def mm(lo, hi):
        return _dot(xb, w_ref[:, lo:hi])

    zq = mm(0, 3 * A)
    qb = (zq[:, :A] * scale).astype(BF16)
    q_ref[...] = qb
    k = zq[:, A:2 * A]
    v = zq[:, 2 * A:3 * A]
    if sample:
        k_ref[...] = k
        v_ref[...] = v
        kb_ref[...] = k.astype(BF16)
    else:
        kt = k.T
        k_ref[0] = kt
        v_ref[0] = v.T
        kb_ref[0] = kt.astype(BF16)
        qf = qb.astype(F32)
        kf = k.astype(BF16).astype(F32)
        feat = lax.broadcasted_iota(jnp.int32, (A, LANES), 0) // (A // H)
        col = lax.broadcasted_iota(jnp.int32, (A, LANES), 1)
        stats = jnp.zeros((tm, LANES), F32)
        for i, prod in enumerate((qf * qf, kf * kf, qf * kf)):
            stats = stats + _dot(prod.astype(BF16), jnp.where(col == feat + i * H, 1.0, 0.0).astype(BF16))
        stats_ref[...] = stats
    vb_ref[...] = v.astype(BF16)

    o = 3 * A
    zc = mm(o, o + 3 * C)
    xin, bg, cg = zc[:, :C], zc[:, C:2 * C], zc[:, 2 * C:]
    u = cg * xin
    row = lax.broadcasted_iota(jnp.int32, (tm, C), 0)
    shifted = [u]
    if sample:
        pos = row % T
        for s in range(1, halo + 1):
            shifted.append(jnp.where(pos >= s, pltpu.roll(u, s, 0), halo_refs[s - 1][...]))
        u_ref[...] = u
    else:
        t = pl.program_id(0) % tiles_per_seq

        @pl.when(t == 0)
        def _():
            carry_scr[...] = jnp.zeros_like(carry_scr)

        for s in range(1, halo + 1):
            us = pltpu.roll(u, s, 0)
            for r in range(s):
                us = jnp.where(row == r, carry_scr[halo - s + r:halo - s + r + 1, :], us)
            shifted.append(us)
    y = cw_ref[0:1, :] * shifted[halo]
    for i in range(1, halo + 1):
        y = y + cw_ref[i:i + 1, :] * shifted[halo - i]
    cy_ref[...] = (bg * y).astype(BF16)
    if not sample:
        carry_scr[0:halo, :] = u[tm - halo:tm, :]

        @pl.when(t == cst_tile)
        def _():
            cst_ref[0] = u[cst_row:cst_row + halo, :]

    o += 3 * C
    zg = mm(o, o + 2 * D)
    sga_ref[...] = jax.nn.sigmoid(zg[:, :D]).astype(BF16)
    sgc_ref[...] = jax.nn.sigmoid(zg[:, D:]).astype(BF16)

    o += 2 * D
    zf = mm(o, o + LANES)
    lf_ref[...] = _log_sigmoid(zf + bf_ref[...])[:, :H]


def _inproj(x, g, w, bf, cw, *, A, C, H, scale, seq_len=None, valid_len=None, T=None, halos=None, tm_target=384):
    M, D = x.shape
    halo = cw.shape[0] - 1
    sample = halos is not None
    if sample:
        tm = M
        tiles_per_seq, cst_tile, cst_row = 1, 0, 0
        assert T >= halo
    else:
        tm = _pick_tile(seq_len, tm_target, 16)
        tiles_per_seq = seq_len // tm
        cst_tile, cst_row = divmod(valid_len - halo, tm)
        assert cst_row + halo <= tm
    n_seq = M // (tiles_per_seq * tm)
    Nw = w.shape[1]

    def row_spec(n):
        return pl.BlockSpec((tm, n), lambda i: (i, 0))

    def full_spec(a):
        return pl.BlockSpec(a.shape, lambda i: (0,) * a.ndim)

    in_specs = [row_spec(D), full_spec(g), pl.BlockSpec((D, Nw), lambda i: (0, 0), pipeline_mode=pl.Buffered(1)),
                full_spec(bf), full_spec(cw)]
    args = [x, g, w, bf, cw]
    if sample:
        in_specs += [row_spec(C) for _ in halos]
        args += list(halos)
    if sample:
        kv_dims = (M, A)
        kv_spec = row_spec(A)
    else:
        assert tm % LANES == 0
        kv_dims = (n_seq, A, seq_len)
        kv_spec = pl.BlockSpec((1, A, tm), lambda i: (i // tiles_per_seq, 0, i % tiles_per_seq))
    out_shape = [jax.ShapeDtypeStruct((M, A), BF16), jax.ShapeDtypeStruct(kv_dims, F32),
                 jax.ShapeDtypeStruct(kv_dims, F32), jax.ShapeDtypeStruct(kv_dims, BF16),
                 jax.ShapeDtypeStruct((M, A), BF16), jax.ShapeDtypeStruct((M, H), F32),
                 jax.ShapeDtypeStruct((M, C), BF16), jax.ShapeDtypeStruct((M, D), BF16),
                 jax.ShapeDtypeStruct((M, D), BF16)]
    out_specs = [row_spec(A), kv_spec, kv_spec, kv_spec, row_spec(A), row_spec(H), row_spec(C), row_spec(D),
                 row_spec(D)]
    scratch = []
    if sample:
        out_shape.append(jax.ShapeDtypeStruct((M, C), F32))
        out_specs.append(row_spec(C))
    else:
        assert 3 * H <= LANES
        out_shape += [jax.ShapeDtypeStruct((n_seq, halo, C), F32), jax.ShapeDtypeStruct((M, LANES), F32)]
        out_specs += [pl.BlockSpec((1, halo, C), lambda i: (i // tiles_per_seq, 0, 0)), row_spec(LANES)]
        scratch.append(pltpu.VMEM((8, C), F32))
    kern = functools.partial(_inproj_kernel, tm=tm, A=A, C=C, D=D, H=H, scale=scale, halo=halo, sample=sample,
                             tiles_per_seq=tiles_per_seq, T=T, cst_tile=cst_tile, cst_row=cst_row)
    return pl.pallas_call(
        kern, grid=(M // tm,), in_specs=in_specs, out_specs=out_specs, out_shape=out_shape,
        scratch_shapes=scratch, compiler_params=_params(("arbitrary",)),
        name="inproj_sample" if sample else "inproj_prompt")(*args)


def _tri_and_ones():
    i = lax.broadcasted_iota(jnp.int32, (LANES, LANES), 0)
    j = lax.broadcasted_iota(jnp.int32, (LANES, LANES), 1)
    return jnp.where(i <= j, 1.0, 0.0).astype(BF16), jnp.ones((LANES, LANES), BF16)


def _cumsum_kernel(x_ref, o_ref, *, n_chunks, R):
    tri, ones = _tri_and_ones()
    x = x_ref[...].reshape(n_chunks * R, LANES)
    hi, mid, lo = _split3(x)
    y = _dot(hi, tri) + _dot(mid, tri) + _dot(lo, tri)
    tot = _dot(hi, ones) + _dot(mid, ones) + _dot(lo, ones)
    carry = jnp.zeros((R, LANES), F32)
    for c in range(n_chunks):
        o_ref[c] = (y[c * R:(c + 1) * R] + carry) * LOG2E
        carry = carry + tot[c * R:(c + 1) * R]


def _cumsum_chunks(x):
    n_chunks, R, _ = x.shape
    return pl.pallas_call(
        functools.partial(_cumsum_kernel, n_chunks=n_chunks, R=R),
        out_shape=jax.ShapeDtypeStruct(x.shape, F32), name="logf_cumsum")(x)


def _attn_kernel(first_ref, q_ref, kt_ref, v_ref, fk_ref, o_ref, qm_scr, m_scr, l_scr, acc_scr, *, tq, tk, n_super,
                 H, Dh):
    qi = pl.program_id(1)
    r = tq // tk
    n_full = qi * r
    n_sub = tk // LANES
    G = LANES // Dh
    lane = lax.broadcasted_iota(jnp.int32, (tq, LANES), 1)

    def group(h):
        return slice((h // G) * LANES, (h // G + 1) * LANES)

    def own_lanes(h):
        return (lane >= (h % G) * Dh) & (lane < (h % G + 1) * Dh)

    for h in range(H):
        qg = q_ref[0, :, group(h)]
        qm_scr[h] = jnp.where(own_lanes(h), qg, jnp.zeros_like(qg))

    def update(chunks, masked):
        starts = [pl.multiple_of(j * tk, tk) for j in chunks]
        for h in range(H):
            parts = []
            for start in starts:
                s = _dot(qm_scr[h], kt_ref[0, group(h), pl.ds(start, tk)])
                s = s - fk_ref[0, h:h + 1, pl.ds(start, tk)]
                if masked:
                    qpos = qi * tq + lax.broadcasted_iota(jnp.int32, (tq, tk), 0)
                    kpos = start + lax.broadcasted_iota(jnp.int32, (tq, tk), 1)
                    s = jnp.where(kpos <= qpos, s, NEG_BIG)
                parts += [s[:, c * LANES:(c + 1) * LANES] for c in range(n_sub)]
            m_old = m_scr[h]
            row_max = jnp.max(functools.reduce(jnp.maximum, parts), axis=1, keepdims=True)
            m_new = jnp.maximum(m_old, jnp.broadcast_to(row_max, (tq, LANES)))
            alpha = jnp.exp2(m_old - m_new)
            probs = [jnp.exp2(part - m_new) for part in parts]
            l_scr[h] = alpha * l_scr[h] + functools.reduce(jnp.add, probs)
            pv = [_dot(jnp.concatenate(probs[i * n_sub:(i + 1) * n_sub], axis=1).astype(BF16),
                       v_ref[0, pl.ds(start, tk), group(h)]) for i, start in enumerate(starts)]
            acc_scr[h] = alpha * acc_scr[h] + functools.reduce(jnp.add, pv)
            m_scr[h] = m_new

    m_scr[...] = jnp.full_like(m_scr, NEG_BIG)
    l_scr[...] = jnp.zeros_like(l_scr)
    acc_scr[...] = jnp.zeros_like(acc_scr)

    first = first_ref[pl.program_id(0) * pl.num_programs(1) + qi]
    count = n_full - first

    def body(jj, carry):
        update([first + n_super * jj + i for i in range(n_super)], False)
        return carry

    lax.fori_loop(0, count // n_super, body, 0)
    done = first + (count // n_super) * n_super
    size = n_super // 2
    while size >= 1:
        take = (count & size) != 0

        @pl.when(take)
        def _(done=done, size=size):
            update([done + i for i in range(size)], False)

        done = done + jnp.where(take, size, 0)
        size //= 2
    for d in range(r):
        update([n_full + d], True)

    for g in range(H // G):
        out = jnp.zeros((tq, LANES), F32)
        for h in range(g * G, (g + 1) * G):
            l = jnp.sum(l_scr[h], axis=1, keepdims=True)
            out = jnp.where(own_lanes(h), acc_scr[h] / l, out)
        o_ref[0, :, group(g * G)] = out.astype(BF16)


SKIP_LOG2 = 160.0
NORM_SLACK = 1.05


def _first_live_chunk(stats, fk, *, H, tq, tk):
    B, Lp, _ = stats.shape
    n_q, n_k = Lp // tq, Lp // tk
    fk_rows = fk.transpose(0, 2, 1)
    q_norm = jnp.sqrt(stats[..., 0:H]).reshape(B, n_q, tq, H).max(axis=2)
    k_norm = jnp.sqrt(stats[..., H:2 * H]).reshape(B, n_k, tk, H).max(axis=2)
    own = (stats[..., 2 * H:3 * H] - fk_rows).reshape(B, n_q, tq, H).min(axis=2)
    fk_min = fk_rows.reshape(B, n_k, tk, H).min(axis=2)
    bound = (NORM_SLACK * q_norm[:, :, None] * k_norm[:, None] - fk_min[:, None] - own[:, :, None])
    live = jnp.any(bound >= -SKIP_LOG2, axis=-1)
    live = live | (jnp.arange(n_k)[None, None, :] * tk + tk > jnp.arange(n_q)[None, :, None] * tq)
    return jnp.argmax(live, axis=-1).astype(jnp.int32).reshape(B * n_q)


def _attn_prompt(q, kt, v, fk, stats, *, H, Dh, tq=256, tk=256, n_super=8):
    B, Lp, A = q.shape
    assert LANES % Dh == 0 and H % (LANES // Dh) == 0 and n_super & (n_super - 1) == 0 and tq == tk
    first = _first_live_chunk(stats, fk, H=H, tq=tq, tk=tk)
    stat = pltpu.VMEM((H, tq, LANES), F32)
    grid_spec = pltpu.PrefetchScalarGridSpec(
        num_scalar_prefetch=1, grid=(B, Lp // tq),
        in_specs=[pl.BlockSpec((1, tq, A), lambda b, i, first: (b, i, 0)),
                  pl.BlockSpec((1, A, Lp), lambda b, i, first: (b, 0, 0), pipeline_mode=pl.Buffered(1)),
                  pl.BlockSpec((1, Lp, A), lambda b, i, first: (b, 0, 0), pipeline_mode=pl.Buffered(1)),
                  pl.BlockSpec((1, H, Lp), lambda b, i, first: (b, 0, 0))],
        out_specs=pl.BlockSpec((1, tq, A), lambda b, i, first: (b, i, 0)),
        scratch_shapes=[pltpu.VMEM((H, tq, LANES), BF16), stat, stat, stat])
    return pl.pallas_call(
        functools.partial(_attn_kernel, tq=tq, tk=tk, n_super=n_super, H=H, Dh=Dh), grid_spec=grid_spec,
        out_shape=jax.ShapeDtypeStruct((B, Lp, A), BF16),
        compiler_params=_params(("arbitrary", "arbitrary")), name="attn_prompt")(first, q, kt, v, fk)


def _attn_sample_kernel(pt_ref, qbd_ref, knew_ref, vnew_ref, lfnew_ref, *refs, PP, T, H, P):
    k_refs, v_refs, lf_refs = refs[:PP], refs[PP:2 * PP], refs[2 * PP:3 * PP]
    o_ref, m_scr, l_scr, acc_scr, carry_scr = refs[3 * PP:]
    j = pl.program_id(1)
    R = T * H
    contract_last = (((1,), (1,)), ((), ()))

    @pl.when(j == 0)
    def _():
        m_scr[...] = jnp.full_like(m_scr, NEG_BIG)
        l_scr[...] = jnp.zeros_like(l_scr)
        acc_scr[...] = jnp.zeros_like(acc_scr)
        carry_scr[...] = jnp.zeros_like(carry_scr)

    tri_ones = jnp.concatenate(_tri_and_ones(), axis=1)
    qbd = qbd_ref[0]

    def pages(state, lfs, kts, vts, mask):
        m, l, acc, carry = state
        n = len(lfs)
        pad = (-H) % 16
        Hp = H + pad
        zeros = [jnp.zeros((pad, P), F32)] if pad else []
        terms = _split3(jnp.concatenate([piece for lf in lfs for piece in [lf] + zeros], axis=0))
        y = _dot(jnp.concatenate(terms, axis=0), tri_ones)
        y = y[0:n * Hp] + y[n * Hp:2 * n * Hp] + y[2 * n * Hp:3 * n * Hp]
        ss = []
        for i in range(n):
            fk = carry + y[i * Hp:i * Hp + H, :P]
            carry = carry + y[i * Hp:i * Hp + H, P:]
            s = _dot(qbd, kts[i]) - jnp.concatenate([fk * LOG2E] * T, axis=0)
            ss.append(s if mask is None else jnp.where(mask, s, NEG_BIG))
        m_new = jnp.maximum(m, jnp.max(functools.reduce(jnp.maximum, ss), axis=1, keepdims=True))
        alpha = jnp.exp2(m - m_new)
        ps = [jnp.exp2(s - m_new) for s in ss]
        l = alpha * l + jnp.sum(functools.reduce(jnp.add, ps), axis=1, keepdims=True)
        pv = [lax.dot_general(p.astype(BF16), vt, contract_last, preferred_element_type=F32)
              for p, vt in zip(ps, vts)]
        acc = alpha * acc + functools.reduce(jnp.add, pv)
        return m_new, l, acc, carry

    state = (m_scr[...], l_scr[...], acc_scr[...], carry_scr[...])
    state = pages(state, [r[0, 0] for r in lf_refs], [r[0, 0].astype(BF16) for r in k_refs],
                  [r[0, 0].astype(BF16) for r in v_refs], None)
    m_scr[...], l_scr[...], acc_scr[...], carry_scr[...] = state

    @pl.when(j == pl.num_programs(1) - 1)
    def _():
        t_row = lax.broadcasted_iota(jnp.int32, (R, P), 0) // H
        key = lax.broadcasted_iota(jnp.int32, (R, P), 1)
        _, l, acc, _ = pages(state, [lfnew_ref[0]], [knew_ref[0]], [vnew_ref[0]], key <= t_row)
        o_ref[0] = acc / l


def _attn_sample(page_table, qbd, k_new, v_new, lf_new, cache_k, cache_v, cache_lf, *, layer, T, H, pages_per_step=16):
    n_seq, n_pages = page_table.shape
    _, _, A, P = cache_k.shape
    R = T * H
    PP = _pick_tile(n_pages, pages_per_step, 1)

    def seq_spec(shape):
        return pl.BlockSpec((1,) + shape, lambda s, j, pt: (s, 0, 0))

    def page_spec(shape, p):
        return pl.BlockSpec((1, 1) + shape, lambda s, j, pt: (layer, pt[s, j * PP + p], 0, 0))

    in_specs = ([seq_spec((R, A)), seq_spec((A, P)), seq_spec((A, P)), seq_spec((H, P))]
                + [page_spec((A, P), p) for p in range(PP)] + [page_spec((A, P), p) for p in range(PP)]
                + [page_spec((H, P), p) for p in range(PP)])
    grid_spec = pltpu.PrefetchScalarGridSpec(
        num_scalar_prefetch=1, grid=(n_seq, n_pages // PP), in_specs=in_specs,
        out_specs=pl.BlockSpec((1, R, A), lambda s, j, pt: (s, 0, 0)),
        scratch_shapes=[pltpu.VMEM((R, 1), F32), pltpu.VMEM((R, 1), F32), pltpu.VMEM((R, A), F32),
                        pltpu.VMEM((H, P), F32)])
    return pl.pallas_call(
        functools.partial(_attn_sample_kernel, PP=PP, T=T, H=H, P=P), grid_spec=grid_spec,
        out_shape=jax.ShapeDtypeStruct((n_seq, R, A), F32),
        compiler_params=_params(("arbitrary", "arbitrary")), name="attn_sample")(
            page_table, qbd, k_new, v_new, lf_new, *([cache_k] * PP), *([cache_v] * PP), *([cache_lf] * PP))


def _merge_kernel(o_ref, cy_ref, sga_ref, sgc_ref, x_ref, wa_ref, wc_ref, wo_ref, out_ref):
    ya = _dot(o_ref[...], wa_ref[...])
    yc = _dot(cy_ref[...], wc_ref[...])
    mix = (sga_ref[...].astype(F32) * ya + sgc_ref[...].astype(F32) * yc).astype(BF16)
    out_ref[...] = x_ref[...] + _dot(mix, wo_ref[...])


def _merge(o, cy, sga, sgc, x, wa, wc, wo, *, tm_target=512):
    M, D = x.shape
    tm = _pick_tile(M, tm_target, 16)

    def row_spec(n):
        return pl.BlockSpec((tm, n), lambda i: (i, 0))

    def w_spec(a):
        return pl.BlockSpec(a.shape, lambda i: (0, 0))

    return pl.pallas_call(
        _merge_kernel, grid=(M // tm,),
        in_specs=[row_spec(o.shape[1]), row_spec(cy.shape[1]), row_spec(D), row_spec(D), row_spec(D),
                  w_spec(wa), w_spec(wc), w_spec(wo)],
        out_specs=row_spec(D), out_shape=jax.ShapeDtypeStruct((M, D), F32),
        compiler_params=_params(("arbitrary",)), name="merge")(o, cy, sga, sgc, x, wa, wc, wo)


def _swiglu_step(xb, wg, wu, wd, row_scale=None):
    gate = _dot(xb, wg)
    up = _dot(xb, wu)
    h = jax.nn.silu(gate) * up
    if row_scale is not None:
        h = h * row_scale
    return _dot(h.astype(BF16), wd)


def _finish(x, acc, gfin_ref, out_ref):
    y = x + acc
    if gfin_ref is not None:
        y = _rms(y, gfin_ref[...])
    out_ref[...] = y


def _ffn_kernel(*refs, final):
    if final:
        x_ref, g_ref, wg_ref, wu_ref, wd_ref, gfin_ref, out_ref, xn_scr, acc_scr = refs
    else:
        x_ref, g_ref, wg_ref, wu_ref, wd_ref, out_ref, xn_scr, acc_scr = refs
        gfin_ref = None
    f = pl.program_id(1)

    @pl.when(f == 0)
    def _():
        xn_scr[...] = _rms(x_ref[...], g_ref[...]).astype(BF16)
        acc_scr[...] = jnp.zeros_like(acc_scr)

    acc_scr[...] += _swiglu_step(xn_scr[...], wg_ref[...], wu_ref[...], wd_ref[...])

    @pl.when(f == pl.num_programs(1) - 1)
    def _():
        _finish(x_ref[...], acc_scr[...], gfin_ref, out_ref)


def _ffn(x, g, wg, wu, wd, gfin=None, *, tm_target=1024, tf_target=512):
    M, D = x.shape
    F = wg.shape[1]
    tm = _pick_tile(M, tm_target, 16)
    tf = _pick_tile(F, tf_target, LANES)
    final = gfin is not None
    in_specs = [pl.BlockSpec((tm, D), lambda i, f: (i, 0)), pl.BlockSpec((1, D), lambda i, f: (0, 0)),
                pl.BlockSpec((D, tf), lambda i, f: (0, f)), pl.BlockSpec((D, tf), lambda i, f: (0, f)),
                pl.BlockSpec((tf, D), lambda i, f: (f, 0))]
    args = [x, g, wg, wu, wd]
    if final:
        in_specs.append(pl.BlockSpec((1, D), lambda i, f: (0, 0)))
        args.append(gfin)
    return pl.pallas_call(
        functools.partial(_ffn_kernel, final=final), grid=(M // tm, F // tf), in_specs=in_specs,
        out_specs=pl.BlockSpec((tm, D), lambda i, f: (i, 0)), out_shape=jax.ShapeDtypeStruct((M, D), F32),
        scratch_shapes=[pltpu.VMEM((tm, D), BF16), pltpu.VMEM((tm, D), F32)],
        compiler_params=_params(("arbitrary", "arbitrary")), name="ffn_dense")(*args)


def _route(xn, wr_hi, wr_lo, n_experts):
    hi = xn.astype(BF16)
    lo = (xn - hi.astype(F32)).astype(BF16)
    logits = _dot(hi, wr_hi) + _dot(lo, wr_hi) + _dot(hi, wr_lo)
    lane = lax.broadcasted_iota(jnp.int32, logits.shape, 1).astype(F32)
    lg = jnp.where(lane < n_experts, logits, NEG_BIG)
    v1 = jnp.max(lg, axis=1, keepdims=True)
    i1 = jnp.min(jnp.where(lg == v1, lane, float(LANES)), axis=1, keepdims=True)
    lg2 = jnp.where(lane == i1, NEG_BIG, lg)
    v2 = jnp.max(lg2, axis=1, keepdims=True)
    i2 = jnp.min(jnp.where(lg2 == v2, lane, float(LANES)), axis=1, keepdims=True)
    e2 = jnp.exp(v2 - v1)
    den = 1.0 + e2
    return jnp.where(lane == i1, 1.0 / den, 0.0) + jnp.where(lane == i2, e2 / den, 0.0)


def _moe_dense_kernel(*refs, final, n_experts):
    if final:
        x_ref, g_ref, wrh_ref, wrl_ref, wg_ref, wu_ref, wd_ref, gfin_ref, out_ref, xn_scr, acc_scr, comb_scr = refs
    else:
        x_ref, g_ref, wrh_ref, wrl_ref, wg_ref, wu_ref, wd_ref, out_ref, xn_scr, acc_scr, comb_scr = refs
        gfin_ref = None
    e = pl.program_id(1)
    f = pl.program_id(2)

    @pl.when((e == 0) & (f == 0))
    def _():
        xn = _rms(x_ref[...], g_ref[...])
        xn_scr[...] = xn.astype(BF16)
        comb_scr[...] = _route(xn, wrh_ref[...], wrl_ref[...], n_experts)
        acc_scr[...] = jnp.zeros_like(acc_scr)

    comb = comb_scr[...]
    lane = lax.broadcasted_iota(jnp.int32, comb.shape, 1)
    ce = jnp.sum(jnp.where(lane == e, comb, 0.0), axis=1, keepdims=True)
    acc_scr[...] += _swiglu_step(xn_scr[...], wg_ref[0], wu_ref[0], wd_ref[0], row_scale=ce)

    @pl.when((e == pl.num_programs(1) - 1) & (f == pl.num_programs(2) - 1))
    def _():
        _finish(x_ref[...], acc_scr[...], gfin_ref, out_ref)


def _moe_dense(x, g, wr_hi, wr_lo, wg, wu, wd, gfin=None, *, tm_target=1024, tf_target=512):
    M, D = x.shape
    E, _, F = wg.shape
    tm = _pick_tile(M, tm_target, 16)
    tf = _pick_tile(F, tf_target, LANES)
    final = gfin is not None
    vec = pl.BlockSpec((1, D), lambda i, e, f: (0, 0))
    wr_spec = pl.BlockSpec((D, LANES), lambda i, e, f: (0, 0))
    in_specs = [pl.BlockSpec((tm, D), lambda i, e, f: (i, 0)), vec, wr_spec, wr_spec,
                pl.BlockSpec((1, D, tf), lambda i, e, f: (e, 0, f)), pl.BlockSpec((1, D, tf), lambda i, e, f: (e, 0, f)),
                pl.BlockSpec((1, tf, D), lambda i, e, f: (e, f, 0))]
    args = [x, g, wr_hi, wr_lo, wg, wu, wd]
    if final:
        in_specs.append(vec)
        args.append(gfin)
    return pl.pallas_call(
        functools.partial(_moe_dense_kernel, final=final, n_experts=E), grid=(M // tm, E, F // tf),
        in_specs=in_specs, out_specs=pl.BlockSpec((tm, D), lambda i, e, f: (i, 0)),
        out_shape=jax.ShapeDtypeStruct((M, D), F32),
        scratch_shapes=[pltpu.VMEM((tm, D), BF16), pltpu.VMEM((tm, D), F32), pltpu.VMEM((tm, LANES), F32)],
        compiler_params=_params(("arbitrary", "arbitrary", "arbitrary")), name="moe_dense")(*args)


SEG_ROWS = 16


def _selected(comb):
    return jnp.where(comb > 0.0, 1.0, 0.0)


def _positions(sel_bf16, tm):
    r = lax.broadcasted_iota(jnp.int32, (tm, tm), 0)
    c = lax.broadcasted_iota(jnp.int32, (tm, tm), 1)
    return _dot(jnp.where(c < r, 1.0, 0.0).astype(BF16), sel_bf16)


def _moe_route_kernel(x_ref, g_ref, wrh_ref, wrl_ref, xn_ref, comb_ref, cnt_ref, *, tm, n_experts, seq_len, valid_len):
    xn = _rms(x_ref[...], g_ref[...])
    xn_ref[...] = xn.astype(BF16)
    comb = _route(xn, wrh_ref[...], wrl_ref[...], n_experts)
    row = pl.program_id(0) * tm + lax.broadcasted_iota(jnp.int32, (tm, 1), 0)
    comb = jnp.where(row % seq_len < valid_len, comb, 0.0)
    comb_ref[...] = comb
    cnt_ref[0] = jnp.sum(_selected(comb), axis=0, keepdims=True)


def _segment_copies(meta, tile, n_experts, make_copy, wait):
    cnt_ref, loff_ref, goff_ref = meta
    for e in range(n_experts):
        idx = tile * n_experts + e
        local0, global0 = loff_ref[idx], goff_ref[idx]

        def body(k, carry, local0=local0, global0=global0):
            cp = make_copy(pl.multiple_of(local0 + k * SEG_ROWS, SEG_ROWS),
                           pl.multiple_of(global0 + k * SEG_ROWS, SEG_ROWS))
            if wait:
                cp.wait()
            else:
                cp.start()
            return carry

        lax.fori_loop(0, cnt_ref[idx] // SEG_ROWS, body, 0)


def _moe_dispatch_kernel(cnt_ref, loff_ref, goff_ref, xn_ref, comb_ref, loffv_ref, xs_in_ref, xs_ref,
                         stage_scr, sem, *, tm, S, n_experts):
    del xs_in_ref
    tile = pl.program_id(0)
    sel = _selected(comb_ref[...])
    slot = loffv_ref[0] + _positions(sel.astype(BF16), tm)
    slot_t = jnp.where(sel > 0.0, slot, -1.0).T
    rows = lax.broadcasted_iota(jnp.int32, (S, tm), 0).astype(F32)
    onehot = jnp.zeros((S, tm), F32)
    for e in range(n_experts):
        onehot = onehot + jnp.where(slot_t[e:e + 1, :] == rows, 1.0, 0.0)
    stage_scr[...] = _dot(onehot.astype(BF16), xn_ref[...]).astype(BF16)

    def make_copy(local, glob):
        return pltpu.make_async_copy(stage_scr.at[pl.ds(local, SEG_ROWS)], xs_ref.at[pl.ds(glob, SEG_ROWS)], sem)

    meta = (cnt_ref, loff_ref, goff_ref)
    _segment_copies(meta, tile, n_experts, make_copy, wait=False)
    _segment_copies(meta, tile, n_experts, make_copy, wait=True)


def _moe_group_kernel(blk_ref, nact_ref, xs_ref, wg_ref, wu_ref, wd_ref, ys_ref, acc_scr):
    del blk_ref
    i = pl.program_id(0)
    f = pl.program_id(1)
    active = i < nact_ref[0]

    @pl.when(f == 0)
    def _():
        acc_scr[...] = jnp.zeros_like(acc_scr)

    @pl.when(active)
    def _():
        acc_scr[...] += _swiglu_step(xs_ref[...], wg_ref[0], wu_ref[0], wd_ref[0])

    @pl.when(f == pl.num_programs(1) - 1)
    def _():
        ys_ref[...] = acc_scr[...].astype(BF16)


def _moe_combine_kernel(cnt_ref, loff_ref, goff_ref, x_ref, comb_ref, loffv_ref, ys_ref, *refs,
                        tm, S, n_experts, final):
    if final:
        gfin_ref, out_ref, stage_scr, sem = refs
    else:
        out_ref, stage_scr, sem = refs
        gfin_ref = None
    tile = pl.program_id(0)

    @pl.when(tile == 0)
    def _():
        stage_scr[...] = jnp.zeros_like(stage_scr)

    def make_copy(local, glob):
        return pltpu.make_async_copy(ys_ref.at[pl.ds(glob, SEG_ROWS)], stage_scr.at[pl.ds(local, SEG_ROWS)], sem)

    meta = (cnt_ref, loff_ref, goff_ref)
    _segment_copies(meta, tile, n_experts, make_copy, wait=False)

    comb = comb_ref[...]
    lane = lax.broadcasted_iota(jnp.int32, comb.shape, 1).astype(F32)
    slot = loffv_ref[0] + _positions(_selected(comb).astype(BF16), tm)
    cols = lax.broadcasted_iota(jnp.int32, (tm, S), 1).astype(F32)
    g1 = jnp.max(comb, axis=1, keepdims=True)
    i1 = jnp.min(jnp.where(comb == g1, lane, float(LANES)), axis=1, keepdims=True)
    rest = jnp.where(lane == i1, -1.0, comb)
    g2 = jnp.max(rest, axis=1, keepdims=True)
    i2 = jnp.min(jnp.where(rest == g2, lane, float(LANES)), axis=1, keepdims=True)
    picks = []
    for gate, idx in ((g1, i1), (g2, i2)):
        s_k = jnp.sum(jnp.where(lane == idx, slot, 0.0), axis=1, keepdims=True)
        picks.append((gate, jnp.where(cols == s_k, 1.0, 0.0).astype(BF16)))

    _segment_copies(meta, tile, n_experts, make_copy, wait=True)
    stage = stage_scr[...]
    acc = picks[0][0] * _dot(picks[0][1], stage) + picks[1][0] * _dot(picks[1][1], stage)
    _finish(x_ref[...], acc, gfin_ref, out_ref)


def _moe_routed(x, g, wr_hi, wr_lo, wg, wu, wd, gfin=None, *, seq_len, valid_len, tm_target=512, block_rows=1024,
                tf_target=512):
    M, D = x.shape
    E, _, F = wg.shape
    tm = _pick_tile(M, tm_target, LANES)
    n_tiles = M // tm
    S = TOP_K * tm + SEG_ROWS * E
    BR = block_rows
    tf = _pick_tile(F, tf_target, LANES)
    n_rows = -(-(TOP_K * M + n_tiles * E * (SEG_ROWS - 1) + E * (BR - 1)) // BR) * BR
    n_blocks = n_rows // BR
    final = gfin is not None

    def tile_spec(n):
        return pl.BlockSpec((tm, n), lambda i, *_: (i, 0))

    def const_spec(a):
        return pl.BlockSpec(a.shape, lambda i, *_: (0,) * a.ndim)

    xn, comb, cnt = pl.pallas_call(
        functools.partial(_moe_route_kernel, tm=tm, n_experts=E, seq_len=seq_len, valid_len=valid_len),
        grid=(n_tiles,),
        in_specs=[tile_spec(D), const_spec(g), const_spec(wr_hi), const_spec(wr_lo)],
        out_specs=[tile_spec(D), tile_spec(LANES), pl.BlockSpec((1, 1, LANES), lambda i: (i, 0, 0))],
        out_shape=[jax.ShapeDtypeStruct((M, D), BF16), jax.ShapeDtypeStruct((M, LANES), F32),
                   jax.ShapeDtypeStruct((n_tiles, 1, LANES), F32)],
        compiler_params=_params(("arbitrary",)), name="moe_route")(x, g, wr_hi, wr_lo)

    cnt = cnt[:, 0, :E].astype(jnp.int32)
    cnt_pad = -(-cnt // SEG_ROWS) * SEG_ROWS
    loff = jnp.cumsum(cnt_pad, axis=1) - cnt_pad
    region = -(-jnp.sum(cnt_pad, axis=0) // BR) * BR
    ends = jnp.cumsum(region)
    goff = (ends - region)[None, :] + jnp.cumsum(cnt_pad, axis=0) - cnt_pad
    blk_expert = jnp.minimum(jnp.searchsorted(ends // BR, jnp.arange(n_blocks), side="right"), E - 1).astype(jnp.int32)
    n_active = (ends[-1:] // BR).astype(jnp.int32)
    meta = (cnt_pad.reshape(-1), loff.reshape(-1), goff.reshape(-1))
    loff_vec = jnp.pad(loff.astype(F32), ((0, 0), (0, LANES - E))).reshape(n_tiles, 1, LANES)
    loffv_spec = pl.BlockSpec((1, 1, LANES), lambda i, *_: (i, 0, 0))
    any_spec = pl.BlockSpec(memory_space=pl.ANY)

    xs = pl.pallas_call(
        functools.partial(_moe_dispatch_kernel, tm=tm, S=S, n_experts=E),
        grid_spec=pltpu.PrefetchScalarGridSpec(
            num_scalar_prefetch=3, grid=(n_tiles,),
            in_specs=[tile_spec(D), tile_spec(LANES), loffv_spec, any_spec], out_specs=any_spec,
            scratch_shapes=[pltpu.VMEM((S, D), BF16), pltpu.SemaphoreType.DMA(())]),
        out_shape=jax.ShapeDtypeStruct((n_rows, D), BF16), input_output_aliases={6: 0},
        compiler_params=_params(("arbitrary",)), name="moe_dispatch")(
            *meta, xn, comb, loff_vec, jnp.zeros((n_rows, D), BF16))

    ys = pl.pallas_call(
        _moe_group_kernel,
        grid_spec=pltpu.PrefetchScalarGridSpec(
            num_scalar_prefetch=2, grid=(n_blocks, F // tf),
            in_specs=[pl.BlockSpec((BR, D), lambda i, f, blk, nact: (i, 0)),
                      pl.BlockSpec((1, D, tf), lambda i, f, blk, nact: (blk[i], 0, f)),
                      pl.BlockSpec((1, D, tf), lambda i, f, blk, nact: (blk[i], 0, f)),
                      pl.BlockSpec((1, tf, D), lambda i, f, blk, nact: (blk[i], f, 0))],
            out_specs=pl.BlockSpec((BR, D), lambda i, f, blk, nact: (i, 0)),
            scratch_shapes=[pltpu.VMEM((BR, D), F32)]),
        out_shape=jax.ShapeDtypeStruct((n_rows, D), BF16),
        compiler_params=_params(("arbitrary", "arbitrary")), name="moe_group")(blk_expert, n_active, xs, wg, wu, wd)

    in_specs = [tile_spec(D), tile_spec(LANES), loffv_spec, any_spec]
    args = [x, comb, loff_vec, ys]
    if final:
        in_specs.append(const_spec(gfin))
        args.append(gfin)
    return pl.pallas_call(
        functools.partial(_moe_combine_kernel, tm=tm, S=S, n_experts=E, final=final),
        grid_spec=pltpu.PrefetchScalarGridSpec(
            num_scalar_prefetch=3, grid=(n_tiles,), in_specs=in_specs, out_specs=tile_spec(D),
            scratch_shapes=[pltpu.VMEM((S, D), BF16), pltpu.SemaphoreType.DMA(())]),
        out_shape=jax.ShapeDtypeStruct((M, D), F32),
        compiler_params=_params(("arbitrary",)), name="moe_combine")(*meta, *args)


def kernel(x_prompt, x_sample, cache_k, cache_v, cache_logf, state_conv, page_table, meta_tokens, norm_mix, w_in,
           b_forget, conv_w, w_out_attn, w_out_conv, w_o, norm_ffn, ffn_w_gate, ffn_w_up, ffn_w_down, moe_router,
           moe_w_gate, moe_w_up, moe_w_down, norm_final):
    B, S, D = x_prompt.shape
    NB, T, _ = x_sample.shape
    depth, n_pool, P, H, Dh = cache_k.shape
    n_meta = meta_tokens.shape[0]
    A = H * Dh
    C = conv_w.shape[2]
    halo = conv_w.shape[1] - 1
    E = moe_router.shape[2]
    L = S + n_meta
    Lp = -(-L // 256) * 256
    n_chunks = Lp // LANES
    scale = float(Dh) ** -0.5 * LOG2E

    o_f = 3 * A
    o_c = o_f + H

    def prep_w_in(w):
        wf = jnp.pad(w[:, o_f:o_c], ((0, 0), (0, LANES - H)))
        return jnp.concatenate([w[:, :o_f], w[:, o_c:], wf], axis=1).astype(BF16)

    def row(v):
        return v.reshape(1, -1)

    meta = jnp.broadcast_to(meta_tokens[None].astype(x_prompt.dtype), (B, n_meta, D))
    x_p = jnp.concatenate([meta, x_prompt, jnp.zeros((B, Lp - L, D), x_prompt.dtype)], axis=1).reshape(B * Lp, D)
    x_s = x_sample.reshape(NB * T, D)
    eye_h = jnp.eye(H, dtype=F32)

    outs = {n: [] for n in ("kp", "vp", "lfp", "cp", "ks", "vs", "lfs", "cs")}
    cache_kt = cache_k.transpose(0, 1, 3, 4, 2).reshape(depth, n_pool, A, P)
    cache_vt = cache_v.transpose(0, 1, 3, 4, 2).reshape(depth, n_pool, A, P)
    cache_lft = cache_logf.transpose(0, 1, 3, 2)
    for l in range(depth):
        last = l == depth - 1
        w_l = prep_w_in(w_in[l])
        bf_l = jnp.pad(b_forget[l], (0, LANES - H)).reshape(1, LANES)
        g_l = row(norm_mix[l])
        wa, wc, wo = w_out_attn[l].astype(BF16), w_out_conv[l].astype(BF16), w_o[l].astype(BF16)

        q, kt, vt, ktb, vb, lf, cy, sga, sgc, cst, stats = _inproj(
            x_p, g_l, w_l, bf_l, conv_w[l], A=A, C=C, H=H, scale=scale, seq_len=Lp, valid_len=L)
        outs["kp"].append(kt[:, :, :L].reshape(B, H, Dh, L))
        outs["vp"].append(vt[:, :, :L].reshape(B, H, Dh, L))
        lf_chunks = lf.reshape(B, n_chunks, LANES, H).transpose(1, 0, 3, 2).reshape(n_chunks, B * H, LANES)
        fk = _cumsum_chunks(lf_chunks).reshape(n_chunks, B, H, LANES).transpose(1, 2, 0, 3).reshape(B, H, Lp)
        o_a = _attn_prompt(q.reshape(B, Lp, A), ktb, vb.reshape(B, Lp, A), fk, stats.reshape(B, Lp, LANES), H=H,
                           Dh=Dh)
        x_p = _merge(o_a.reshape(B * Lp, A), cy, sga, sgc, x_p, wa, wc, wo)
        outs["lfp"].append(lf.reshape(B, Lp, H)[:, :L])
        outs["cp"].append(cst)

        st = state_conv[l]
        halos = [jnp.pad(st[:, halo - s:], ((0, 0), (0, T - s), (0, 0))).reshape(NB * T, C) for s in range(1, halo + 1)]
        q, k, v, kb, vb, lf, cy, sga, sgc, u = _inproj(
            x_s, g_l, w_l, bf_l, conv_w[l], A=A, C=C, H=H, scale=scale, T=T, halos=halos)
        qbd = (q.reshape(NB, T, H, 1, Dh) * eye_h.astype(BF16)[None, None, :, :, None]).reshape(NB, T * H, A)
        pad_keys = ((0, 0), (0, 0), (0, P - T))
        o_bd = _attn_sample(
            page_table, qbd, jnp.pad(kb.reshape(NB, T, A).transpose(0, 2, 1), pad_keys),
            jnp.pad(vb.reshape(NB, T, A).transpose(0, 2, 1), pad_keys),
            jnp.pad(lf.reshape(NB, T, H).transpose(0, 2, 1), pad_keys),
            cache_kt, cache_vt, cache_lft, layer=l, T=T, H=H)
        o_a = jnp.sum(o_bd.reshape(NB, T, H, H, Dh) * eye_h[None, None, :, :, None], axis=2).reshape(NB * T, A)
        x_s = _merge(o_a.astype(BF16), cy, sga, sgc, x_s, wa, wc, wo)
        outs["ks"].append(k.reshape(NB, T, H, Dh))
        outs["vs"].append(v.reshape(NB, T, H, Dh))
        outs["lfs"].append(lf.reshape(NB, T, H))
        outs["cs"].append(u.reshape(NB, T, C)[:, T - halo:])

        gf = row(norm_ffn[l])
        gfin = row(norm_final) if last else None
        j = l // 2
        if l % 2 == 0:
            wg, wu, wd = ffn_w_gate[j].astype(BF16), ffn_w_up[j].astype(BF16), ffn_w_down[j].astype(BF16)
            x_p = _ffn(x_p, gf, wg, wu, wd, gfin)
            x_s = _ffn(x_s, gf, wg, wu, wd, gfin)
        else:
            wr = jnp.pad(moe_router[j], ((0, 0), (0, LANES - E)))
            wr_hi = wr.astype(BF16)
            wr_lo = (wr - wr_hi.astype(F32)).astype(BF16)
            wg, wu, wd = moe_w_gate[j].astype(BF16), moe_w_up[j].astype(BF16), moe_w_down[j].astype(BF16)
            x_p = _moe_routed(x_p, gf, wr_hi, wr_lo, wg, wu, wd, gfin, seq_len=Lp, valid_len=L)
            x_s = _moe_dense(x_s, gf, wr_hi, wr_lo, wg, wu, wd, gfin)

    y_prompt = x_p.reshape(B, Lp, D)[:, n_meta:L]
    y_sample = x_s.reshape(NB, T, D)
    k_prompt = jnp.stack(outs["kp"]).transpose(0, 1, 4, 2, 3)
    v_prompt = jnp.stack(outs["vp"]).transpose(0, 1, 4, 2, 3)
    return (y_prompt, y_sample, k_prompt, v_prompt, jnp.stack(outs["lfp"]),
            jnp.stack(outs["cp"]), jnp.stack(outs["ks"]), jnp.stack(outs["vs"]), jnp.stack(outs["lfs"]),
            jnp.stack(outs["cs"]))
```

```python
import functools

import jax
import jax.numpy as jnp
from jax import lax
from jax.experimental import pallas as pl
from jax.experimental.pallas import tpu as pltpu

RMS_EPS = 1e-6
TOP_K = 2
NEG_BIG = -1e30
LOG2E = 1.4426950408889634
LANES = 128
V7X_VMEM_LIMIT_BYTES = 56 * 1024 * 1024

F32 = jnp.float32
BF16 = jnp.bfloat16


def _pick_tile(n, target, mult=8):
    best = None
    for t in range(mult, min(n, target) + 1, mult):
        if n % t == 0:
            best = t
    assert best is not None, (n, target, mult)
    return best


def _params(sem, vmem=V7X_VMEM_LIMIT_BYTES):
    return pltpu.CompilerParams(dimension_semantics=sem, vmem_limit_bytes=vmem)


def _rms(x, g):
    ms = jnp.mean(x * x, axis=-1, keepdims=True)
    return (x * lax.rsqrt(ms + RMS_EPS)) * g


def _log_sigmoid(x):
    return jnp.minimum(x, 0.0) - jnp.log1p(jnp.exp(-jnp.abs(x)))


def _split3(x):
    hi = x.astype(BF16)
    r1 = x - hi.astype(F32)
    mid = r1.astype(BF16)
    lo = (r1 - mid.astype(F32)).astype(BF16)
    return hi, mid, lo


def _dot(a, b):
    return jnp.dot(a, b, preferred_element_type=F32)


def _dot3(x, w_bf16):
    hi, mid, lo = _split3(x)
    return _dot(hi, w_bf16) + _dot(mid, w_bf16) + _dot(lo, w_bf16)


def _inproj_kernel(*refs, tm, A, C, D, H, scale, halo, sample, tiles_per_seq, T, cst_tile, cst_row):
    if sample:
        (x_ref, g_ref, w_ref, bf_ref, cw_ref, *halo_refs) = refs[: 5 + halo]
        (q_ref, k_ref, v_ref, kb_ref, vb_ref, lf_ref, cy_ref, sga_ref, sgc_ref, u_ref) = refs[5 + halo:]
    else:
        (x_ref, g_ref, w_ref, bf_ref, cw_ref,
         q_ref, k_ref, v_ref, kb_ref, vb_ref, lf_ref, cy_ref, sga_ref, sgc_ref, cst_ref, stats_ref,
         carry_scr) = refs

    xb = _rms(x_ref[...], g_ref[...]).astype(BF16)

    def mm(lo, hi):
        return _dot(xb, w_ref[:, lo:hi])

    zq = mm(0, 3 * A)
    qb = (zq[:, :A] * scale).astype(BF16)
    q_ref[...] = qb
    k = zq[:, A:2 * A]
    v = zq[:, 2 * A:3 * A]
    if sample:
        k_ref[...] = k
        v_ref[...] = v
        kb_ref[...] = k.astype(BF16)
    else:
        kt = k.T
        k_ref[0] = kt
        v_ref[0] = v.T
        kb_ref[0] = kt.astype(BF16)
        qf = qb.astype(F32)
        kf = k.astype(BF16).astype(F32)
        feat = lax.broadcasted_iota(jnp.int32, (A, LANES), 0) // (A // H)
        col = lax.broadcasted_iota(jnp.int32, (A, LANES), 1)
        stats = jnp.zeros((tm, LANES), F32)
        for i, prod in enumerate((qf * qf, kf * kf, qf * kf)):
            stats = stats + _dot(prod.astype(BF16), jnp.where(col == feat + i * H, 1.0, 0.0).astype(BF16))
        stats_ref[...] = stats
    vb_ref[...] = v.astype(BF16)

    o = 3 * A
    zc = mm(o, o + 3 * C)
    xin, bg, cg = zc[:, :C], zc[:, C:2 * C], zc[:, 2 * C:]
    u = cg * xin
    row = lax.broadcasted_iota(jnp.int32, (tm, C), 0)
    shifted = [u]
    if sample:
        pos = row % T
        for s in range(1, halo + 1):
            shifted.append(jnp.where(pos >= s, pltpu.roll(u, s, 0), halo_refs[s - 1][...]))
        u_ref[...] = u
    else:
        t = pl.program_id(0) % tiles_per_seq

        @pl.when(t == 0)
        def _():
            carry_scr[...] = jnp.zeros_like(carry_scr)

        for s in range(1, halo + 1):
            us = pltpu.roll(u, s, 0)
            for r in range(s):
                us = jnp.where(row == r, carry_scr[halo - s + r:halo - s + r + 1, :], us)
            shifted.append(us)
    y = cw_ref[0:1, :] * shifted[halo]
    for i in range(1, halo + 1):
        y = y + cw_ref[i:i + 1, :] * shifted[halo - i]
    cy_ref[...] = (bg * y).astype(BF16)
    if not sample:
        carry_scr[0:halo, :] = u[tm - halo:tm, :]

        @pl.when(t == cst_tile)
        def _():
            cst_ref[0] = u[cst_row:cst_row + halo, :]

    o += 3 * C
    zg = mm(o, o + 2 * D)
    sga_ref[...] = jax.nn.sigmoid(zg[:, :D]).astype(BF16)
    sgc_ref[...] = jax.nn.sigmoid(zg[:, D:]).astype(BF16)

    o += 2 * D
    zf = mm(o, o + LANES)
    lf_ref[...] = _log_sigmoid(zf + bf_ref[...])[:, :H]


def _inproj(x, g, w, bf, cw, *, A, C, H, scale, seq_len=None, valid_len=None, T=None, halos=None, tm_target=384):
    M, D = x.shape
    halo = cw.shape[0] - 1
    sample = halos is not None
    if sample:
        tm = M
        tiles_per_seq, cst_tile, cst_row = 1, 0, 0
        assert T >= halo
    else:
        tm = _pick_tile(seq_len, tm_target, 16)
        tiles_per_seq = seq_len // tm
        cst_tile, cst_row = divmod(valid_len - halo, tm)
        assert cst_row + halo <= tm
    n_seq = M // (tiles_per_seq * tm)
    Nw = w.shape[1]

    def row_spec(n):
        return pl.BlockSpec((tm, n), lambda i: (i, 0))

    def full_spec(a):
        return pl.BlockSpec(a.shape, lambda i: (0,) * a.ndim)

    in_specs = [row_spec(D), full_spec(g), pl.BlockSpec((D, Nw), lambda i: (0, 0), pipeline_mode=pl.Buffered(1)),
                full_spec(bf), full_spec(cw)]
    args = [x, g, w, bf, cw]
    if sample:
        in_specs += [row_spec(C) for _ in halos]
        args += list(halos)
    if sample:
        kv_dims = (M, A)
        kv_spec = row_spec(A)
    else:
        assert tm % LANES == 0
        kv_dims = (n_seq, A, seq_len)
        kv_spec = pl.BlockSpec((1, A, tm), lambda i: (i // tiles_per_seq, 0, i % tiles_per_seq))
    out_shape = [jax.ShapeDtypeStruct((M, A), BF16), jax.ShapeDtypeStruct(kv_dims, F32),
                 jax.ShapeDtypeStruct(kv_dims, F32), jax.ShapeDtypeStruct(kv_dims, BF16),
                 jax.ShapeDtypeStruct((M, A), BF16), jax.ShapeDtypeStruct((M, H), F32),
                 jax.ShapeDtypeStruct((M, C), BF16), jax.ShapeDtypeStruct((M, D), BF16),
                 jax.ShapeDtypeStruct((M, D), BF16)]
    out_specs = [row_spec(A), kv_spec, kv_spec, kv_spec, row_spec(A), row_spec(H), row_spec(C), row_spec(D),
                 row_spec(D)]
    scratch = []
    if sample:
        out_shape.append(jax.ShapeDtypeStruct((M, C), F32))
        out_specs.append(row_spec(C))
    else:
        assert 3 * H <= LANES
        out_shape += [jax.ShapeDtypeStruct((n_seq, halo, C), F32), jax.ShapeDtypeStruct((M, LANES), F32)]
        out_specs += [pl.BlockSpec((1, halo, C), lambda i: (i // tiles_per_seq, 0, 0)), row_spec(LANES)]
        scratch.append(pltpu.VMEM((8, C), F32))
    kern = functools.partial(_inproj_kernel, tm=tm, A=A, C=C, D=D, H=H, scale=scale, halo=halo, sample=sample,
                             tiles_per_seq=tiles_per_seq, T=T, cst_tile=cst_tile, cst_row=cst_row)
    return pl.pallas_call(
        kern, grid=(M // tm,), in_specs=in_specs, out_specs=out_specs, out_shape=out_shape,
        scratch_shapes=scratch, compiler_params=_params(("arbitrary",)),
        name="inproj_sample" if sample else "inproj_prompt")(*args)


def _tri_and_ones():
    i = lax.broadcasted_iota(jnp.int32, (LANES, LANES), 0)
    j = lax.broadcasted_iota(jnp.int32, (LANES, LANES), 1)
    return jnp.where(i <= j, 1.0, 0.0).astype(BF16), jnp.ones((LANES, LANES), BF16)


def _cumsum_kernel(x_ref, o_ref, *, n_chunks, R):
    tri, ones = _tri_and_ones()
    x = x_ref[...].reshape(n_chunks * R, LANES)
    hi, mid, lo = _split3(x)
    y = _dot(hi, tri) + _dot(mid, tri) + _dot(lo, tri)
    tot = _dot(hi, ones) + _dot(mid, ones) + _dot(lo, ones)
    carry = jnp.zeros((R, LANES), F32)
    for c in range(n_chunks):
        o_ref[c] = (y[c * R:(c + 1) * R] + carry) * LOG2E
        carry = carry + tot[c * R:(c + 1) * R]


def _cumsum_chunks(x):
    n_chunks, R, _ = x.shape
    return pl.pallas_call(
        functools.partial(_cumsum_kernel, n_chunks=n_chunks, R=R),
        out_shape=jax.ShapeDtypeStruct(x.shape, F32), name="logf_cumsum")(x)


def _attn_kernel(first_ref, q_ref, kt_ref, v_ref, fk_ref, o_ref, qm_scr, m_scr, l_scr, acc_scr, *, tq, tk, n_super,
                 H, Dh):
    qi = pl.program_id(1)
    r = tq // tk
    n_full = qi * r
    n_sub = tk // LANES
    G = LANES // Dh
    lane = lax.broadcasted_iota(jnp.int32, (tq, LANES), 1)

    def group(h):
        return slice((h // G) * LANES, (h // G + 1) * LANES)

    def own_lanes(h):
        return (lane >= (h % G) * Dh) & (lane < (h % G + 1) * Dh)

    for h in range(H):
        qg = q_ref[0, :, group(h)]
        qm_scr[h] = jnp.where(own_lanes(h), qg, jnp.zeros_like(qg))

    def update(chunks, masked):
        starts = [pl.multiple_of(j * tk, tk) for j in chunks]
        for h in range(H):
            parts = []
            for start in starts:
                s = _dot(qm_scr[h], kt_ref[0, group(h), pl.ds(start, tk)])
                s = s - fk_ref[0, h:h + 1, pl.ds(start, tk)]
                if masked:
                    qpos = qi * tq + lax.broadcasted_iota(jnp.int32, (tq, tk), 0)
                    kpos = start + lax.broadcasted_iota(jnp.int32, (tq, tk), 1)
                    s = jnp.where(kpos <= qpos, s, NEG_BIG)
                parts += [s[:, c * LANES:(c + 1) * LANES] for c in range(n_sub)]
            m_old = m_scr[h]
            row_max = jnp.max(functools.reduce(jnp.maximum, parts), axis=1, keepdims=True)
            m_new = jnp.maximum(m_old, jnp.broadcast_to(row_max, (tq, LANES)))
            alpha = jnp.exp2(m_old - m_new)
            probs = [jnp.exp2(part - m_new) for part in parts]
            l_scr[h] = alpha * l_scr[h] + functools.reduce(jnp.add, probs)
            pv = [_dot(jnp.concatenate(probs[i * n_sub:(i + 1) * n_sub], axis=1).astype(BF16),
                       v_ref[0, pl.ds(start, tk), group(h)]) for i, start in enumerate(starts)]
            acc_scr[h] = alpha * acc_scr[h] + functools.reduce(jnp.add, pv)
            m_scr[h] = m_new

    m_scr[...] = jnp.full_like(m_scr, NEG_BIG)
    l_scr[...] = jnp.zeros_like(l_scr)
    acc_scr[...] = jnp.zeros_like(acc_scr)

    first = first_ref[pl.program_id(0) * pl.num_programs(1) + qi]
    count = n_full - first

    def body(jj, carry):
        update([first + n_super * jj + i for i in range(n_super)], False)
        return carry

    lax.fori_loop(0, count // n_super, body, 0)
    done = first + (count // n_super) * n_super
    size = n_super // 2
    while size >= 1:
        take = (count & size) != 0

        @pl.when(take)
        def _(done=done, size=size):
            update([done + i for i in range(size)], False)

        done = done + jnp.where(take, size, 0)
        size //= 2
    for d in range(r):
        update([n_full + d], True)

    for g in range(H // G):
        out = jnp.zeros((tq, LANES), F32)
        for h in range(g * G, (g + 1) * G):
            l = jnp.sum(l_scr[h], axis=1, keepdims=True)
            out = jnp.where(own_lanes(h), acc_scr[h] / l, out)
        o_ref[0, :, group(g * G)] = out.astype(BF16)


SKIP_LOG2 = 160.0
NORM_SLACK = 1.05


def _first_live_chunk(stats, fk, *, H, tq, tk):
    B, Lp, _ = stats.shape
    n_q, n_k = Lp // tq, Lp // tk
    fk_rows = fk.transpose(0, 2, 1)
    q_norm = jnp.sqrt(stats[..., 0:H]).reshape(B, n_q, tq, H).max(axis=2)
    k_norm = jnp.sqrt(stats[..., H:2 * H]).reshape(B, n_k, tk, H).max(axis=2)
    own = (stats[..., 2 * H:3 * H] - fk_rows).reshape(B, n_q, tq, H).min(axis=2)
    fk_min = fk_rows.reshape(B, n_k, tk, H).min(axis=2)
    bound = (NORM_SLACK * q_norm[:, :, None] * k_norm[:, None] - fk_min[:, None] - own[:, :, None])
    live = jnp.any(bound >= -SKIP_LOG2, axis=-1)
    live = live | (jnp.arange(n_k)[None, None, :] * tk + tk > jnp.arange(n_q)[None, :, None] * tq)
    first = jnp.min(jnp.where(live, jnp.arange(n_k, dtype=jnp.int32)[None, None, :], n_k), axis=-1)
    return first.astype(jnp.int32).reshape(B * n_q)


def _attn_prompt(q, kt, v, fk, stats, *, H, Dh, tq=256, tk=256, n_super=8):
    B, Lp, A = q.shape
    assert LANES % Dh == 0 and H % (LANES // Dh) == 0 and n_super & (n_super - 1) == 0 and tq == tk
    first = _first_live_chunk(stats, fk, H=H, tq=tq, tk=tk)
    stat = pltpu.VMEM((H, tq, LANES), F32)
    grid_spec = pltpu.PrefetchScalarGridSpec(
        num_scalar_prefetch=1, grid=(B, Lp // tq),
        in_specs=[pl.BlockSpec((1, tq, A), lambda b, i, first: (b, i, 0)),
                  pl.BlockSpec((1, A, Lp), lambda b, i, first: (b, 0, 0), pipeline_mode=pl.Buffered(1)),
                  pl.BlockSpec((1, Lp, A), lambda b, i, first: (b, 0, 0), pipeline_mode=pl.Buffered(1)),
                  pl.BlockSpec((1, H, Lp), lambda b, i, first: (b, 0, 0))],
        out_specs=pl.BlockSpec((1, tq, A), lambda b, i, first: (b, i, 0)),
        scratch_shapes=[pltpu.VMEM((H, tq, LANES), BF16), stat, stat, stat])
    return pl.pallas_call(
        functools.partial(_attn_kernel, tq=tq, tk=tk, n_super=n_super, H=H, Dh=Dh), grid_spec=grid_spec,
        out_shape=jax.ShapeDtypeStruct((B, Lp, A), BF16),
        compiler_params=_params(("arbitrary", "arbitrary")), name="attn_prompt")(first, q, kt, v, fk)


def _attn_sample_kernel(pt_ref, qbd_ref, knew_ref, vnew_ref, lfnew_ref, *refs, PP, T, H, P):
    k_refs, v_refs, lf_refs = refs[:PP], refs[PP:2 * PP], refs[2 * PP:3 * PP]
    o_ref, m_scr, l_scr, acc_scr, carry_scr = refs[3 * PP:]
    j = pl.program_id(1)
    R = T * H
    contract_last = (((1,), (1,)), ((), ()))

    @pl.when(j == 0)
    def _():
        m_scr[...] = jnp.full_like(m_scr, NEG_BIG)
        l_scr[...] = jnp.zeros_like(l_scr)
        acc_scr[...] = jnp.zeros_like(acc_scr)
        carry_scr[...] = jnp.zeros_like(carry_scr)

    tri_ones = jnp.concatenate(_tri_and_ones(), axis=1)
    qbd = qbd_ref[0]

    def pages(state, lfs, kts, vts, mask):
        m, l, acc, carry = state
        n = len(lfs)
        pad = (-H) % 16
        Hp = H + pad
        zeros = [jnp.zeros((pad, P), F32)] if pad else []
        terms = _split3(jnp.concatenate([piece for lf in lfs for piece in [lf] + zeros], axis=0))
        y = _dot(jnp.concatenate(terms, axis=0), tri_ones)
        y = y[0:n * Hp] + y[n * Hp:2 * n * Hp] + y[2 * n * Hp:3 * n * Hp]
        ss = []
        for i in range(n):
            fk = carry + y[i * Hp:i * Hp + H, :P]
            carry = carry + y[i * Hp:i * Hp + H, P:]
            s = _dot(qbd, kts[i]) - jnp.concatenate([fk * LOG2E] * T, axis=0)
            ss.append(s if mask is None else jnp.where(mask, s, NEG_BIG))
        m_new = jnp.maximum(m, jnp.max(functools.reduce(jnp.maximum, ss), axis=1, keepdims=True))
        alpha = jnp.exp2(m - m_new)
        ps = [jnp.exp2(s - m_new) for s in ss]
        l = alpha * l + jnp.sum(functools.reduce(jnp.add, ps), axis=1, keepdims=True)
        pv = [lax.dot_general(p.astype(BF16), vt, contract_last, preferred_element_type=F32)
              for p, vt in zip(ps, vts)]
        acc = alpha * acc + functools.reduce(jnp.add, pv)
        return m_new, l, acc, carry

    state = (m_scr[...], l_scr[...], acc_scr[...], carry_scr[...])
    state = pages(state, [r[0, 0] for r in lf_refs], [r[0, 0].astype(BF16) for r in k_refs],
                  [r[0, 0].astype(BF16) for r in v_refs], None)
    m_scr[...], l_scr[...], acc_scr[...], carry_scr[...] = state

    @pl.when(j == pl.num_programs(1) - 1)
    def _():
        t_row = lax.broadcasted_iota(jnp.int32, (R, P), 0) // H
        key = lax.broadcasted_iota(jnp.int32, (R, P), 1)
        _, l, acc, _ = pages(state, [lfnew_ref[0]], [knew_ref[0]], [vnew_ref[0]], key <= t_row)
        o_ref[0] = acc / l


def _attn_sample(page_table, qbd, k_new, v_new, lf_new, cache_k, cache_v, cache_lf, *, layer, T, H, pages_per_step=16):
    n_seq, n_pages = page_table.shape
    _, _, A, P = cache_k.shape
    R = T * H
    PP = _pick_tile(n_pages, pages_per_step, 1)

    def seq_spec(shape):
        return pl.BlockSpec((1,) + shape, lambda s, j, pt: (s, 0, 0))

    def page_spec(shape, p):
        return pl.BlockSpec((1, 1) + shape, lambda s, j, pt: (layer, pt[s, j * PP + p], 0, 0))

    in_specs = ([seq_spec((R, A)), seq_spec((A, P)), seq_spec((A, P)), seq_spec((H, P))]
                + [page_spec((A, P), p) for p in range(PP)] + [page_spec((A, P), p) for p in range(PP)]
                + [page_spec((H, P), p) for p in range(PP)])
    grid_spec = pltpu.PrefetchScalarGridSpec(
        num_scalar_prefetch=1, grid=(n_seq, n_pages // PP), in_specs=in_specs,
        out_specs=pl.BlockSpec((1, R, A), lambda s, j, pt: (s, 0, 0)),
        scratch_shapes=[pltpu.VMEM((R, 1), F32), pltpu.VMEM((R, 1), F32), pltpu.VMEM((R, A), F32),
                        pltpu.VMEM((H, P), F32)])
    return pl.pallas_call(
        functools.partial(_attn_sample_kernel, PP=PP, T=T, H=H, P=P), grid_spec=grid_spec,
        out_shape=jax.ShapeDtypeStruct((n_seq, R, A), F32),
        compiler_params=_params(("arbitrary", "arbitrary")), name="attn_sample")(
            page_table, qbd, k_new, v_new, lf_new, *([cache_k] * PP), *([cache_v] * PP), *([cache_lf] * PP))


def _merge_kernel(o_ref, cy_ref, sga_ref, sgc_ref, x_ref, wa_ref, wc_ref, wo_ref, out_ref):
    ya = _dot(o_ref[...], wa_ref[...])
    yc = _dot(cy_ref[...], wc_ref[...])
    mix = (sga_ref[...].astype(F32) * ya + sgc_ref[...].astype(F32) * yc).astype(BF16)
    out_ref[...] = x_ref[...] + _dot(mix, wo_ref[...])


def _merge(o, cy, sga, sgc, x, wa, wc, wo, *, tm_target=512):
    M, D = x.shape
    tm = _pick_tile(M, tm_target, 16)

    def row_spec(n):
        return pl.BlockSpec((tm, n), lambda i: (i, 0))

    def w_spec(a):
        return pl.BlockSpec(a.shape, lambda i: (0, 0))

    return pl.pallas_call(
        _merge_kernel, grid=(M // tm,),
        in_specs=[row_spec(o.shape[1]), row_spec(cy.shape[1]), row_spec(D), row_spec(D), row_spec(D),
                  w_spec(wa), w_spec(wc), w_spec(wo)],
        out_specs=row_spec(D), out_shape=jax.ShapeDtypeStruct((M, D), F32),
        compiler_params=_params(("arbitrary",)), name="merge")(o, cy, sga, sgc, x, wa, wc, wo)


def _swiglu_step(xb, wg, wu, wd, row_scale=None):
    gate = _dot(xb, wg)
    up = _dot(xb, wu)
    h = jax.nn.silu(gate) * up
    if row_scale is not None:
        h = h * row_scale
    return _dot(h.astype(BF16), wd)


def _finish(x, acc, gfin_ref, out_ref):
    y = x + acc
    if gfin_ref is not None:
        y = _rms(y, gfin_ref[...])
    out_ref[...] = y


def _ffn_kernel(*refs, final):
    if final:
        x_ref, g_ref, wg_ref, wu_ref, wd_ref, gfin_ref, out_ref, xn_scr, acc_scr = refs
    else:
        x_ref, g_ref, wg_ref, wu_ref, wd_ref, out_ref, xn_scr, acc_scr = refs
        gfin_ref = None
    f = pl.program_id(1)

    @pl.when(f == 0)
    def _():
        xn_scr[...] = _rms(x_ref[...], g_ref[...]).astype(BF16)
        acc_scr[...] = jnp.zeros_like(acc_scr)

    acc_scr[...] += _swiglu_step(xn_scr[...], wg_ref[...], wu_ref[...], wd_ref[...])

    @pl.when(f == pl.num_programs(1) - 1)
    def _():
        _finish(x_ref[...], acc_scr[...], gfin_ref, out_ref)


def _ffn(x, g, wg, wu, wd, gfin=None, *, tm_target=1024, tf_target=512):
    M, D = x.shape
    F = wg.shape[1]
    tm = _pick_tile(M, tm_target, 16)
    tf = _pick_tile(F, tf_target, LANES)
    final = gfin is not None
    in_specs = [pl.BlockSpec((tm, D), lambda i, f: (i, 0)), pl.BlockSpec((1, D), lambda i, f: (0, 0)),
                pl.BlockSpec((D, tf), lambda i, f: (0, f)), pl.BlockSpec((D, tf), lambda i, f: (0, f)),
                pl.BlockSpec((tf, D), lambda i, f: (f, 0))]
    args = [x, g, wg, wu, wd]
    if final:
        in_specs.append(pl.BlockSpec((1, D), lambda i, f: (0, 0)))
        args.append(gfin)
    return pl.pallas_call(
        functools.partial(_ffn_kernel, final=final), grid=(M // tm, F // tf), in_specs=in_specs,
        out_specs=pl.BlockSpec((tm, D), lambda i, f: (i, 0)), out_shape=jax.ShapeDtypeStruct((M, D), F32),
        scratch_shapes=[pltpu.VMEM((tm, D), BF16), pltpu.VMEM((tm, D), F32)],
        compiler_params=_params(("arbitrary", "arbitrary")), name="ffn_dense")(*args)


def _route(xn, wr_hi, wr_lo, n_experts):
    hi = xn.astype(BF16)
    lo = (xn - hi.astype(F32)).astype(BF16)
    logits = _dot(hi, wr_hi) + _dot(lo, wr_hi) + _dot(hi, wr_lo)
    lane = lax.broadcasted_iota(jnp.int32, logits.shape, 1).astype(F32)
    lg = jnp.where(lane < n_experts, logits, NEG_BIG)
    v1 = jnp.max(lg, axis=1, keepdims=True)
    i1 = jnp.min(jnp.where(lg == v1, lane, float(LANES)), axis=1, keepdims=True)
    lg2 = jnp.where(lane == i1, NEG_BIG, lg)
    v2 = jnp.max(lg2, axis=1, keepdims=True)
    i2 = jnp.min(jnp.where(lg2 == v2, lane, float(LANES)), axis=1, keepdims=True)
    e2 = jnp.exp(v2 - v1)
    den = 1.0 + e2
    return jnp.where(lane == i1, 1.0 / den, 0.0) + jnp.where(lane == i2, e2 / den, 0.0)


def _moe_dense_kernel(*refs, final, n_experts):
    if final:
        x_ref, g_ref, wrh_ref, wrl_ref, wg_ref, wu_ref, wd_ref, gfin_ref, out_ref, xn_scr, acc_scr, comb_scr = refs
    else:
        x_ref, g_ref, wrh_ref, wrl_ref, wg_ref, wu_ref, wd_ref, out_ref, xn_scr, acc_scr, comb_scr = refs
        gfin_ref = None
    e = pl.program_id(1)
    f = pl.program_id(2)

    @pl.when((e == 0) & (f == 0))
    def _():
        xn = _rms(x_ref[...], g_ref[...])
        xn_scr[...] = xn.astype(BF16)
        comb_scr[...] = _route(xn, wrh_ref[...], wrl_ref[...], n_experts)
        acc_scr[...] = jnp.zeros_like(acc_scr)

    comb = comb_scr[...]
    lane = lax.broadcasted_iota(jnp.int32, comb.shape, 1)
    ce = jnp.sum(jnp.where(lane == e, comb, 0.0), axis=1, keepdims=True)
    acc_scr[...] += _swiglu_step(xn_scr[...], wg_ref[0], wu_ref[0], wd_ref[0], row_scale=ce)

    @pl.when((e == pl.num_programs(1) - 1) & (f == pl.num_programs(2) - 1))
    def _():
        _finish(x_ref[...], acc_scr[...], gfin_ref, out_ref)


def _moe_dense(x, g, wr_hi, wr_lo, wg, wu, wd, gfin=None, *, tm_target=1024, tf_target=512):
    M, D = x.shape
    E, _, F = wg.shape
    tm = _pick_tile(M, tm_target, 16)
    tf = _pick_tile(F, tf_target, LANES)
    final = gfin is not None
    vec = pl.BlockSpec((1, D), lambda i, e, f: (0, 0))
    wr_spec = pl.BlockSpec((D, LANES), lambda i, e, f: (0, 0))
    in_specs = [pl.BlockSpec((tm, D), lambda i, e, f: (i, 0)), vec, wr_spec, wr_spec,
                pl.BlockSpec((1, D, tf), lambda i, e, f: (e, 0, f)), pl.BlockSpec((1, D, tf), lambda i, e, f: (e, 0, f)),
                pl.BlockSpec((1, tf, D), lambda i, e, f: (e, f, 0))]
    args = [x, g, wr_hi, wr_lo, wg, wu, wd]
    if final:
        in_specs.append(vec)
        args.append(gfin)
    return pl.pallas_call(
        functools.partial(_moe_dense_kernel, final=final, n_experts=E), grid=(M // tm, E, F // tf),
        in_specs=in_specs, out_specs=pl.BlockSpec((tm, D), lambda i, e, f: (i, 0)),
        out_shape=jax.ShapeDtypeStruct((M, D), F32),
        scratch_shapes=[pltpu.VMEM((tm, D), BF16), pltpu.VMEM((tm, D), F32), pltpu.VMEM((tm, LANES), F32)],
        compiler_params=_params(("arbitrary", "arbitrary", "arbitrary")), name="moe_dense")(*args)


SEG_ROWS = 16


def _selected(comb):
    return jnp.where(comb > 0.0, 1.0, 0.0)


def _positions(sel_bf16, tm):
    r = lax.broadcasted_iota(jnp.int32, (tm, tm), 0)
    c = lax.broadcasted_iota(jnp.int32, (tm, tm), 1)
    return _dot(jnp.where(c < r, 1.0, 0.0).astype(BF16), sel_bf16)


def _moe_route_kernel(x_ref, g_ref, wrh_ref, wrl_ref, xn_ref, comb_ref, cnt_ref, *, tm, n_experts, seq_len, valid_len):
    xn = _rms(x_ref[...], g_ref[...])
    xn_ref[...] = xn.astype(BF16)
    comb = _route(xn, wrh_ref[...], wrl_ref[...], n_experts)
    row = pl.program_id(0) * tm + lax.broadcasted_iota(jnp.int32, (tm, 1), 0)
    comb = jnp.where(row % seq_len < valid_len, comb, 0.0)
    comb_ref[...] = comb
    cnt_ref[0] = jnp.sum(_selected(comb), axis=0, keepdims=True)


def _segment_copies(meta, tile, n_experts, make_copy, wait):
    cnt_ref, loff_ref, goff_ref = meta
    for e in range(n_experts):
        idx = tile * n_experts + e
        local0, global0 = loff_ref[idx], goff_ref[idx]

        def body(k, carry, local0=local0, global0=global0):
            cp = make_copy(pl.multiple_of(local0 + k * SEG_ROWS, SEG_ROWS),
                           pl.multiple_of(global0 + k * SEG_ROWS, SEG_ROWS))
            if wait:
                cp.wait()
            else:
                cp.start()
            return carry

        lax.fori_loop(0, cnt_ref[idx] // SEG_ROWS, body, 0)


def _moe_dispatch_kernel(cnt_ref, loff_ref, goff_ref, xn_ref, comb_ref, loffv_ref, xs_in_ref, xs_ref,
                         stage_scr, sem, *, tm, S, n_experts):
    del xs_in_ref
    tile = pl.program_id(0)
    sel = _selected(comb_ref[...])
    slot = loffv_ref[0] + _positions(sel.astype(BF16), tm)
    slot_t = jnp.where(sel > 0.0, slot, -1.0).T
    rows = lax.broadcasted_iota(jnp.int32, (S, tm), 0).astype(F32)
    onehot = jnp.zeros((S, tm), F32)
    for e in range(n_experts):
        onehot = onehot + jnp.where(slot_t[e:e + 1, :] == rows, 1.0, 0.0)
    stage_scr[...] = _dot(onehot.astype(BF16), xn_ref[...]).astype(BF16)

    def make_copy(local, glob):
        return pltpu.make_async_copy(stage_scr.at[pl.ds(local, SEG_ROWS)], xs_ref.at[pl.ds(glob, SEG_ROWS)], sem)

    meta = (cnt_ref, loff_ref, goff_ref)
    _segment_copies(meta, tile, n_experts, make_copy, wait=False)
    _segment_copies(meta, tile, n_experts, make_copy, wait=True)


def _moe_group_kernel(blk_ref, nact_ref, xs_ref, wg_ref, wu_ref, wd_ref, ys_ref, acc_scr):
    del blk_ref
    i = pl.program_id(0)
    f = pl.program_id(1)
    active = i < nact_ref[0]

    @pl.when(f == 0)
    def _():
        acc_scr[...] = jnp.zeros_like(acc_scr)

    @pl.when(active)
    def _():
        acc_scr[...] += _swiglu_step(xs_ref[...], wg_ref[0], wu_ref[0], wd_ref[0])

    @pl.when(f == pl.num_programs(1) - 1)
    def _():
        ys_ref[...] = acc_scr[...].astype(BF16)


def _moe_combine_kernel(cnt_ref, loff_ref, goff_ref, x_ref, comb_ref, loffv_ref, ys_ref, *refs,
                        tm, S, n_experts, final):
    if final:
        gfin_ref, out_ref, stage_scr, sem = refs
    else:
        out_ref, stage_scr, sem = refs
        gfin_ref = None
    tile = pl.program_id(0)

    @pl.when(tile == 0)
    def _():
        stage_scr[...] = jnp.zeros_like(stage_scr)

    def make_copy(local, glob):
        return pltpu.make_async_copy(ys_ref.at[pl.ds(glob, SEG_ROWS)], stage_scr.at[pl.ds(local, SEG_ROWS)], sem)

    meta = (cnt_ref, loff_ref, goff_ref)
    _segment_copies(meta, tile, n_experts, make_copy, wait=False)

    comb = comb_ref[...]
    lane = lax.broadcasted_iota(jnp.int32, comb.shape, 1).astype(F32)
    slot = loffv_ref[0] + _positions(_selected(comb).astype(BF16), tm)
    cols = lax.broadcasted_iota(jnp.int32, (tm, S), 1).astype(F32)
    g1 = jnp.max(comb, axis=1, keepdims=True)
    i1 = jnp.min(jnp.where(comb == g1, lane, float(LANES)), axis=1, keepdims=True)
    rest = jnp.where(lane == i1, -1.0, comb)
    g2 = jnp.max(rest, axis=1, keepdims=True)
    i2 = jnp.min(jnp.where(rest == g2, lane, float(LANES)), axis=1, keepdims=True)
    picks = []
    for gate, idx in ((g1, i1), (g2, i2)):
        s_k = jnp.sum(jnp.where(lane == idx, slot, 0.0), axis=1, keepdims=True)
        picks.append((gate, jnp.where(cols == s_k, 1.0, 0.0).astype(BF16)))

    _segment_copies(meta, tile, n_experts, make_copy, wait=True)
    stage = stage_scr[...]
    acc = picks[0][0] * _dot(picks[0][1], stage) + picks[1][0] * _dot(picks[1][1], stage)
    _finish(x_ref[...], acc, gfin_ref, out_ref)


def _moe_routed(x, g, wr_hi, wr_lo, wg, wu, wd, gfin=None, *, seq_len, valid_len, tm_target=512, block_rows=1024,
                tf_target=512):
    M, D = x.shape
    E, _, F = wg.shape
    tm = _pick_tile(M, tm_target, LANES)
    n_tiles = M // tm
    S = TOP_K * tm + SEG_ROWS * E
    BR = block_rows
    tf = _pick_tile(F, tf_target, LANES)
    n_rows = -(-(TOP_K * M + n_tiles * E * (SEG_ROWS - 1) + E * (BR - 1)) // BR) * BR
    n_blocks = n_rows // BR
    final = gfin is not None

    def tile_spec(n):
        return pl.BlockSpec((tm, n), lambda i, *_: (i, 0))

    def const_spec(a):
        return pl.BlockSpec(a.shape, lambda i, *_: (0,) * a.ndim)

    xn, comb, cnt = pl.pallas_call(
        functools.partial(_moe_route_kernel, tm=tm, n_experts=E, seq_len=seq_len, valid_len=valid_len),
        grid=(n_tiles,),
        in_specs=[tile_spec(D), const_spec(g), const_spec(wr_hi), const_spec(wr_lo)],
        out_specs=[tile_spec(D), tile_spec(LANES), pl.BlockSpec((1, 1, LANES), lambda i: (i, 0, 0))],
        out_shape=[jax.ShapeDtypeStruct((M, D), BF16), jax.ShapeDtypeStruct((M, LANES), F32),
                   jax.ShapeDtypeStruct((n_tiles, 1, LANES), F32)],
        compiler_params=_params(("arbitrary",)), name="moe_route")(x, g, wr_hi, wr_lo)

    cnt = cnt[:, 0, :E].astype(jnp.int32)
    cnt_pad = -(-cnt // SEG_ROWS) * SEG_ROWS
    def prefix(x):
        n = x.shape[0]
        earlier = (jnp.arange(n)[None, :] < jnp.arange(n)[:, None]).astype(x.dtype)
        return jnp.sum(earlier.reshape((n, n) + (1,) * (x.ndim - 1)) * x[None], axis=1)

    loff = prefix(cnt_pad.T).T
    region = -(-jnp.sum(cnt_pad, axis=0) // BR) * BR
    ends = prefix(region) + region
    goff = (ends - region)[None, :] + prefix(cnt_pad)
    blk_expert = jnp.sum(jnp.arange(n_blocks)[:, None] >= (ends // BR)[None, :], axis=1)
    blk_expert = jnp.minimum(blk_expert, E - 1).astype(jnp.int32)
    n_active = (ends[-1:] // BR).astype(jnp.int32)
    meta = (cnt_pad.reshape(-1), loff.reshape(-1), goff.reshape(-1))
    loff_vec = jnp.pad(loff.astype(F32), ((0, 0), (0, LANES - E))).reshape(n_tiles, 1, LANES)
    loffv_spec = pl.BlockSpec((1, 1, LANES), lambda i, *_: (i, 0, 0))
    any_spec = pl.BlockSpec(memory_space=pl.ANY)

    xs = pl.pallas_call(
        functools.partial(_moe_dispatch_kernel, tm=tm, S=S, n_experts=E),
        grid_spec=pltpu.PrefetchScalarGridSpec(
            num_scalar_prefetch=3, grid=(n_tiles,),
            in_specs=[tile_spec(D), tile_spec(LANES), loffv_spec, any_spec], out_specs=any_spec,
            scratch_shapes=[pltpu.VMEM((S, D), BF16), pltpu.SemaphoreType.DMA(())]),
        out_shape=jax.ShapeDtypeStruct((n_rows, D), BF16), input_output_aliases={6: 0},
        compiler_params=_params(("arbitrary",)), name="moe_dispatch")(
            *meta, xn, comb, loff_vec, jnp.zeros((n_rows, D), BF16))

    ys = pl.pallas_call(
        _moe_group_kernel,
        grid_spec=pltpu.PrefetchScalarGridSpec(
            num_scalar_prefetch=2, grid=(n_blocks, F // tf),
            in_specs=[pl.BlockSpec((BR, D), lambda i, f, blk, nact: (i, 0)),
                      pl.BlockSpec((1, D, tf), lambda i, f, blk, nact: (blk[i], 0, f)),
                      pl.BlockSpec((1, D, tf), lambda i, f, blk, nact: (blk[i], 0, f)),
                      pl.BlockSpec((1, tf, D), lambda i, f, blk, nact: (blk[i], f, 0))],
            out_specs=pl.BlockSpec((BR, D), lambda i, f, blk, nact: (i, 0)),
            scratch_shapes=[pltpu.VMEM((BR, D), F32)]),
        out_shape=jax.ShapeDtypeStruct((n_rows, D), BF16),
        compiler_params=_params(("arbitrary", "arbitrary")), name="moe_group")(blk_expert, n_active, xs, wg, wu, wd)

    in_specs = [tile_spec(D), tile_spec(LANES), loffv_spec, any_spec]
    args = [x, comb, loff_vec, ys]
    if final:
        in_specs.append(const_spec(gfin))
        args.append(gfin)
    return pl.pallas_call(
        functools.partial(_moe_combine_kernel, tm=tm, S=S, n_experts=E, final=final),
        grid_spec=pltpu.PrefetchScalarGridSpec(
            num_scalar_prefetch=3, grid=(n_tiles,), in_specs=in_specs, out_specs=tile_spec(D),
            scratch_shapes=[pltpu.VMEM((S, D), BF16), pltpu.SemaphoreType.DMA(())]),
        out_shape=jax.ShapeDtypeStruct((M, D), F32),
        compiler_params=_params(("arbitrary",)), name="moe_combine")(*meta, *args)


def kernel(x_prompt, x_sample, cache_k, cache_v, cache_logf, state_conv, page_table, meta_tokens, norm_mix, w_in,
           b_forget, conv_w, w_out_attn, w_out_conv, w_o, norm_ffn, ffn_w_gate, ffn_w_up, ffn_w_down, moe_router,
           moe_w_gate, moe_w_up, moe_w_down, norm_final):
    B, S, D = x_prompt.shape
    NB, T, _ = x_sample.shape
    depth, n_pool, P, H, Dh = cache_k.shape
    n_meta = meta_tokens.shape[0]
    A = H * Dh
    C = conv_w.shape[2]
    halo = conv_w.shape[1] - 1
    E = moe_router.shape[2]
    L = S + n_meta
    Lp = -(-L // 256) * 256
    n_chunks = Lp // LANES
    scale = float(Dh) ** -0.5 * LOG2E

    o_f = 3 * A
    o_c = o_f + H

    def prep_w_in(w):
        wf = jnp.pad(w[:, o_f:o_c], ((0, 0), (0, LANES - H)))
        return jnp.concatenate([w[:, :o_f], w[:, o_c:], wf], axis=1).astype(BF16)

    def row(v):
        return v.reshape(1, -1)

    meta = jnp.broadcast_to(meta_tokens[None].astype(x_prompt.dtype), (B, n_meta, D))
    x_p = jnp.concatenate([meta, x_prompt, jnp.zeros((B, Lp - L, D), x_prompt.dtype)], axis=1).reshape(B * Lp, D)
    x_s = x_sample.reshape(NB * T, D)
    eye_h = jnp.eye(H, dtype=F32)

    outs = {n: [] for n in ("kp", "vp", "lfp", "cp", "ks", "vs", "lfs", "cs")}
    cache_kt = cache_k.transpose(0, 1, 3, 4, 2).reshape(depth, n_pool, A, P)
    cache_vt = cache_v.transpose(0, 1, 3, 4, 2).reshape(depth, n_pool, A, P)
    cache_lft = cache_logf.transpose(0, 1, 3, 2)
    for l in range(depth):
        last = l == depth - 1
        w_l = prep_w_in(w_in[l])
        bf_l = jnp.pad(b_forget[l], (0, LANES - H)).reshape(1, LANES)
        g_l = row(norm_mix[l])
        wa, wc, wo = w_out_attn[l].astype(BF16), w_out_conv[l].astype(BF16), w_o[l].astype(BF16)

        q, kt, vt, ktb, vb, lf, cy, sga, sgc, cst, stats = _inproj(
            x_p, g_l, w_l, bf_l, conv_w[l], A=A, C=C, H=H, scale=scale, seq_len=Lp, valid_len=L)
        outs["kp"].append(kt[:, :, :L].reshape(B, H, Dh, L))
        outs["vp"].append(vt[:, :, :L].reshape(B, H, Dh, L))
        lf_chunks = lf.reshape(B, n_chunks, LANES, H).transpose(1, 0, 3, 2).reshape(n_chunks, B * H, LANES)
        fk = _cumsum_chunks(lf_chunks).reshape(n_chunks, B, H, LANES).transpose(1, 2, 0, 3).reshape(B, H, Lp)
        o_a = _attn_prompt(q.reshape(B, Lp, A), ktb, vb.reshape(B, Lp, A), fk, stats.reshape(B, Lp, LANES), H=H,
                           Dh=Dh)
        x_p = _merge(o_a.reshape(B * Lp, A), cy, sga, sgc, x_p, wa, wc, wo)
        outs["lfp"].append(lf.reshape(B, Lp, H)[:, :L])
        outs["cp"].append(cst)

        st = state_conv[l]
        halos = [jnp.pad(st[:, halo - s:], ((0, 0), (0, T - s), (0, 0))).reshape(NB * T, C) for s in range(1, halo + 1)]
        q, k, v, kb, vb, lf, cy, sga, sgc, u = _inproj(
            x_s, g_l, w_l, bf_l, conv_w[l], A=A, C=C, H=H, scale=scale, T=T, halos=halos)
        qbd = (q.reshape(NB, T, H, 1, Dh) * eye_h.astype(BF16)[None, None, :, :, None]).reshape(NB, T * H, A)
        pad_keys = ((0, 0), (0, 0), (0, P - T))
        o_bd = _attn_sample(
            page_table, qbd, jnp.pad(kb.reshape(NB, T, A).transpose(0, 2, 1), pad_keys),
            jnp.pad(vb.reshape(NB, T, A).transpose(0, 2, 1), pad_keys),
            jnp.pad(lf.reshape(NB, T, H).transpose(0, 2, 1), pad_keys),
            cache_kt, cache_vt, cache_lft, layer=l, T=T, H=H)
        o_a = jnp.sum(o_bd.reshape(NB, T, H, H, Dh) * eye_h[None, None, :, :, None], axis=2).reshape(NB * T, A)
        x_s = _merge(o_a.astype(BF16), cy, sga, sgc, x_s, wa, wc, wo)
        outs["ks"].append(k.reshape(NB, T, H, Dh))
        outs["vs"].append(v.reshape(NB, T, H, Dh))
        outs["lfs"].append(lf.reshape(NB, T, H))
        outs["cs"].append(u.reshape(NB, T, C)[:, T - halo:])

        gf = row(norm_ffn[l])
        gfin = row(norm_final) if last else None
        j = l // 2
        if l % 2 == 0:
            wg, wu, wd = ffn_w_gate[j].astype(BF16), ffn_w_up[j].astype(BF16), ffn_w_down[j].astype(BF16)
            x_p = _ffn(x_p, gf, wg, wu, wd, gfin)
            x_s = _ffn(x_s, gf, wg, wu, wd, gfin)
        else:
            wr = jnp.pad(moe_router[j], ((0, 0), (0, LANES - E)))
            wr_hi = wr.astype(BF16)
            wr_lo = (wr - wr_hi.astype(F32)).astype(BF16)
            wg, wu, wd = moe_w_gate[j].astype(BF16), moe_w_up[j].astype(BF16), moe_w_down[j].astype(BF16)
            x_p = _moe_routed(x_p, gf, wr_hi, wr_lo, wg, wu, wd, gfin, seq_len=Lp, valid_len=L)
            x_s = _moe_dense(x_s, gf, wr_hi, wr_lo, wg, wu, wd, gfin)

    y_prompt = x_p.reshape(B, Lp, D)[:, n_meta:L]
    y_sample = x_s.reshape(NB, T, D)
    k_prompt = jnp.stack(outs["kp"]).transpose(0, 1, 4, 2, 3)
    v_prompt = jnp.stack(outs["vp"]).transpose(0, 1, 4, 2, 3)
    return (y_prompt, y_sample, k_prompt, v_prompt, jnp.stack(outs["lfp"]),
            jnp.stack(outs["cp"]), jnp.stack(outs["ks"]), jnp.stack(outs["vs"]), jnp.stack(outs["lfs"]),
            jnp.stack(outs["cs"]))
```
